```python
import functools
import jax, jax.numpy as jnp
from jax import lax
import numpy as np

D_MODEL = 1024
BATCH = 16
SEQ = 2048
DEPTH = 1
DEC_BATCH = 32
DEC_SEQ = 8
PAST_LEN = 16384
PAGE_SIZE = 128

N_META = 16
H_R = 8
HD_R = 64
C_R = H_R * HD_R
D_W_LORA = 64
D_A_LORA = 64
D_G_LORA = 128
P_R = 3 * C_R + D_W_LORA + D_A_LORA + D_G_LORA
LNX_EPS = 64e-5
H_A = 8
H_KV = 4
HD_A = 64
C_A = H_A * HD_A
C_KV = H_KV * HD_A
H_IDX = 8
D_IDX = 64
TOPK_MAX = 256
P_A = C_A + 2 * C_KV + H_IDX * D_IDX + H_IDX + D_IDX
ROPE_THETA = 500000.0
ROT_FRAC = 4
Q_BLOCK = 128
P_G = 2 * D_MODEL
N_IN = P_R + P_A + P_G
D_FF = ((8 * D_MODEL // 3 + 127) // 128) * 128
CONV_W = 3
RMS_EPS = 1e-6

kernel_name = 'hybrid_rwkv7_dsa_step'


def rms_norm(x, g):
    xf = x.astype(jnp.float32)
    y = xf * lax.rsqrt(jnp.mean(xf * xf, axis=-1, keepdims=True) + RMS_EPS)
    return (y * g.astype(jnp.float32)).astype(x.dtype)


def rope_partial(x, pos):
    d = x.shape[-1]
    rot = d // ROT_FRAC
    half = rot // 2
    inv = ROPE_THETA ** (-jnp.arange(half, dtype=jnp.float32) / half)
    ang = pos.astype(jnp.float32)[:, None] * inv[None, :]
    ang = ang.reshape((1, pos.shape[0]) + (1,) * (x.ndim - 3) + (half,))
    cos, sin = jnp.cos(ang), jnp.sin(ang)
    xf = x.astype(jnp.float32)
    x1, x2, rest = xf[..., :half], xf[..., half:rot], xf[..., rot:]
    out = jnp.concatenate([x1 * cos - x2 * sin, x2 * cos + x1 * sin, rest], axis=-1)
    return out.astype(x.dtype)


def split_cols(p):
    o = 0
    p_r = p[..., o:o + P_R]; o += P_R
    p_a = p[..., o:o + P_A]; o += P_A
    g_r = p[..., o:o + D_MODEL]; o += D_MODEL
    g_a = p[..., o:o + D_MODEL]
    return p_r, p_a, g_r, g_a


def _rwkv7_step(S, inp):
    r, dec, k, v, kk, a = inp
    sa = jnp.einsum('bhvk,bhk->bhv', S, -kk)
    S = S * dec[:, :, None, :] + sa[..., None] * (kk * a)[:, :, None, :] + v[..., None] * k[:, :, None, :]
    y = jnp.einsum('bhvk,bhk->bhv', S, r)
    return S, y


def rwkv7_branch(p_r, shift_prev, s_prev, lp):
    B, T, _ = p_r.shape
    f32 = jnp.float32
    prev = jnp.concatenate([shift_prev[:, None, :].astype(p_r.dtype), p_r[:, :-1]], axis=1)
    pm = p_r + (prev - p_r) * lp['rwkv_mu']
    r = pm[..., :C_R]
    k = pm[..., C_R:2 * C_R]
    v = pm[..., 2 * C_R:3 * C_R]
    o = 3 * C_R
    wd = pm[..., o:o + D_W_LORA]; o += D_W_LORA
    ad = pm[..., o:o + D_A_LORA]; o += D_A_LORA
    gd = pm[..., o:o + D_G_LORA]
    w = -jax.nn.softplus(-(lp['rwkv_w0'] + jnp.tanh(wd) @ lp['rwkv_w_w2'])) - 0.5
    decay = jnp.exp(-jnp.exp(w.astype(f32)))
    a = jax.nn.sigmoid(lp['rwkv_a0'] + ad @ lp['rwkv_w_a2'])
    g = jax.nn.sigmoid(gd) @ lp['rwkv_w_g2']
    heads = lambda t: t.reshape(B, T, H_R, HD_R).astype(f32)
    kk = heads(k * lp['rwkv_k_k'])
    kk = kk * lax.rsqrt(jnp.maximum(jnp.sum(kk * kk, axis=-1, keepdims=True), 1e-24))
    k = k * (1 + (a - 1) * lp['rwkv_k_a'])
    rh, kh, vh, ah, dh = heads(r), heads(k), heads(v), heads(a), heads(decay)
    xs = tuple(jnp.moveaxis(t, 1, 0) for t in (rh, dh, kh, vh, kk, ah))
    s_fin, y = lax.scan(_rwkv7_step, s_prev.astype(f32), xs)
    y = jnp.moveaxis(y, 0, 1)
    mean = jnp.mean(y, axis=-1, keepdims=True)
    var = jnp.mean(jnp.square(y - mean), axis=-1, keepdims=True)
    y = ((y - mean) * lax.rsqrt(var + LNX_EPS)).reshape(B, T, C_R)
    y = y * lp['rwkv_lnx_g'] + lp['rwkv_lnx_b']
    y = y + (jnp.sum(rh * kh * lp['rwkv_r_k'], axis=-1, keepdims=True) * vh).reshape(B, T, C_R)
    out = (y * g.astype(f32)).astype(p_r.dtype)
    return out, s_fin, p_r[:, -1]


def dsa_project(p_a, pos):
    B, T, _ = p_a.shape
    o = 0
    q = p_a[..., o:o + C_A].reshape(B, T, H_A, HD_A); o += C_A
    k = p_a[..., o:o + C_KV].reshape(B, T, H_KV, HD_A); o += C_KV
    v = p_a[..., o:o + C_KV].reshape(B, T, H_KV, HD_A); o += C_KV
    qi = p_a[..., o:o + H_IDX * D_IDX].reshape(B, T, H_IDX, D_IDX); o += H_IDX * D_IDX
    wi = p_a[..., o:o + H_IDX] * (H_IDX ** -0.5); o += H_IDX
    ki = p_a[..., o:o + D_IDX]
    return (rope_partial(q, pos), rope_partial(k, pos), v,
            rope_partial(qi, pos), wi, rope_partial(ki, pos))


def indexer_topk(qi, wi, ki_all, q_pos, n_top):
    sc = jnp.einsum('bqhd,bsd->bqhs', qi, ki_all).astype(jnp.float32) * (D_IDX ** -0.5)
    score = jnp.einsum('bqhs,bqh->bqs', jax.nn.relu(sc), wi.astype(jnp.float32))
    key_pos = jnp.arange(ki_all.shape[1])
    vis = key_pos[None, None, :] <= q_pos[None, :, None]
    score = jnp.where(vis, score, -jnp.inf)
    _, idx = lax.top_k(score, n_top)
    valid = idx <= q_pos[None, :, None]
    return idx, valid


def sparse_attend(q, k_sel, v_sel, valid):
    B, Q = q.shape[:2]
    qg = q.reshape(B, Q, H_KV, H_A // H_KV, HD_A)
    s = jnp.einsum('bqngd,bqknd->bqngk', qg, k_sel).astype(jnp.float32) * (HD_A ** -0.5)
    s = jnp.where(valid[:, :, None, None, :], s, -jnp.inf)
    p = jax.nn.softmax(s, axis=-1)
    o = jnp.einsum('bqngk,bqknd->bqngd', p.astype(v_sel.dtype), v_sel)
    return o.reshape(B, Q, C_A)


def dsa_prompt(q, k, v, qi, wi, ki, n_top):
    B, T = q.shape[:2]
    n_blk = -(-T // Q_BLOCK)
    pad = n_blk * Q_BLOCK - T

    def blocks(t):
        t = jnp.pad(t, [(0, 0), (0, pad)] + [(0, 0)] * (t.ndim - 2))
        return jnp.moveaxis(t.reshape((B, n_blk, Q_BLOCK) + t.shape[2:]), 1, 0)

    pos_b = jnp.arange(n_blk * Q_BLOCK).reshape(n_blk, Q_BLOCK)
    bidx = jnp.arange(B)[:, None, None]

    def one_block(xs):
        qb, qib, wib, pb = xs
        idx, valid = indexer_topk(qib, wib, ki, pb, n_top)
        return sparse_attend(qb, k[bidx, idx], v[bidx, idx], valid)

    o = lax.map(one_block, (blocks(q), blocks(qi), blocks(wi), pos_b))
    return jnp.moveaxis(o, 0, 1).reshape(B, n_blk * Q_BLOCK, C_A)[:, :T]


def dsa_sample(q, k, v, qi, wi, ki, cache_k, cache_v, cache_kidx, page_table):
    B, Q = q.shape[:2]
    past = page_table.shape[1] * PAGE_SIZE
    ki_past = cache_kidx[page_table].reshape(B, past, D_IDX).astype(ki.dtype)
    ki_all = jnp.concatenate([ki_past, ki], axis=1)
    q_pos = past + jnp.arange(Q)
    n_top = min(TOPK_MAX, (past + Q) // 4)
    idx, valid = indexer_topk(qi, wi, ki_all, q_pos, n_top)
    bidx = jnp.arange(B)[:, None, None]
    in_past = idx < past
    s_p = jnp.minimum(idx, past - 1)
    phys = page_table[bidx, s_p // PAGE_SIZE]
    off = s_p % PAGE_SIZE
    s_n = jnp.clip(idx - past, 0, Q - 1)

    def select(pool, new):
        return jnp.where(in_past[..., None, None], pool[phys, off].astype(new.dtype), new[bidx, s_n])

    return sparse_attend(q, select(cache_k, k), select(cache_v, v), valid)


def conv_ffn(h, conv_prev, lp):
    B, T, _ = h.shape
    u = h @ lp['w_up']
    a, b = u[..., :D_FF], u[..., D_FF:]
    ext = jnp.concatenate([conv_prev.astype(a.dtype), a], axis=1)
    c = lp['conv_b'] + ext[:, 0:T] * lp['conv_w'][0]
    for j in range(1, CONV_W):
        c = c + ext[:, j:j + T] * lp['conv_w'][j]
    y = (jax.nn.gelu(c, approximate=False) * b) @ lp['w_down']
    return y, ext[:, T:]


def trunk_layer(x, pos, attend, shift_prev, s_prev, conv_prev, lp):
    h = rms_norm(x, lp['norm1_g']) @ lp['w_in']
    p_r, p_a, g_r, g_a = split_cols(h)
    y_r, s_fin, shift_last = rwkv7_branch(p_r, shift_prev, s_prev, lp)
    q, k, v, qi, wi, ki = dsa_project(p_a, pos)
    y_a = attend(q, k, v, qi, wi, ki)
    merged = (jax.nn.sigmoid(g_r) * (y_r @ lp['w_br_rwkv'])
              + jax.nn.sigmoid(g_a) * (y_a @ lp['w_br_attn']))
    x = x + merged @ lp['w_out']
    f, conv_last = conv_ffn(rms_norm(x, lp['norm2_g']), conv_prev, lp)
    x = x + f
    return x, (k, v, ki, s_fin, shift_last, conv_last)


def setup_inputs(seed: int = 0) -> dict:
    key = jax.random.key(seed)
    keys = iter(jax.random.split(key, 48))
    nrm = lambda shape, scale: jax.random.normal(next(keys), shape, jnp.float32) * scale
    uni = lambda shape, lo, hi: jax.random.uniform(next(keys), shape, jnp.float32, lo, hi)
    n_pages = PAST_LEN // PAGE_SIZE
    n_used = DEC_BATCH * n_pages
    n_pool = n_used + n_used // 4
    perm = jax.random.permutation(next(keys), n_pool)
    page_table = perm[:n_used].reshape(DEC_BATCH, n_pages).astype(jnp.int32)
    L = DEPTH
    return {
        'x_prompt': nrm((BATCH, SEQ, D_MODEL), 1.0),
        'x_sample': nrm((DEC_BATCH, DEC_SEQ, D_MODEL), 1.0),
        'cache_k': nrm((L, n_pool, PAGE_SIZE, H_KV, HD_A), 1.0),
        'cache_v': nrm((L, n_pool, PAGE_SIZE, H_KV, HD_A), 1.0),
        'cache_kidx': nrm((L, n_pool, PAGE_SIZE, D_IDX), 1.0),
        'state_rwkv': nrm((L, DEC_BATCH, H_R, HD_R, HD_R), 0.5),
        'state_rwkv_shift': nrm((L, DEC_BATCH, P_R), 1.0),
        'state_ffn_conv': nrm((L, DEC_BATCH, CONV_W - 1, D_FF), 1.0),
        'page_table': page_table,
        'meta_tokens': nrm((N_META, D_MODEL), 1.0),
        'norm1_g': 1.0 + nrm((L, D_MODEL), 0.02),
        'w_in': nrm((L, D_MODEL, N_IN), D_MODEL ** -0.5),
        'rwkv_mu': uni((L, P_R), 0.0, 1.0),
        'rwkv_w0': uni((L, C_R), -6.0, 0.0),
        'rwkv_w_w2': nrm((L, D_W_LORA, C_R), 0.1),
        'rwkv_a0': nrm((L, C_R), 0.1),
        'rwkv_w_a2': nrm((L, D_A_LORA, C_R), D_A_LORA ** -0.5),
        'rwkv_w_g2': nrm((L, D_G_LORA, C_R), D_G_LORA ** -0.5),
        'rwkv_k_k': 0.85 + nrm((L, C_R), 0.02),
        'rwkv_k_a': 1.0 + nrm((L, C_R), 0.02),
        'rwkv_r_k': nrm((L, H_R, HD_R), 0.1),
        'rwkv_lnx_g': 1.0 + nrm((L, C_R), 0.02),
        'rwkv_lnx_b': nrm((L, C_R), 0.02),
        'w_br_rwkv': nrm((L, C_R, D_MODEL), C_R ** -0.5),
        'w_br_attn': nrm((L, C_A, D_MODEL), C_A ** -0.5),
        'w_out': nrm((L, D_MODEL, D_MODEL), D_MODEL ** -0.5),
        'norm2_g': 1.0 + nrm((L, D_MODEL), 0.02),
        'w_up': nrm((L, D_MODEL, 2 * D_FF), D_MODEL ** -0.5),
        'conv_w': nrm((L, CONV_W, D_FF), CONV_W ** -0.5),
        'conv_b': nrm((L, D_FF), 0.02),
        'w_down': nrm((L, D_FF, D_MODEL), D_FF ** -0.5),
        'final_norm_g': 1.0 + nrm((D_MODEL,), 0.02),
    }


def reference(x_prompt, x_sample, cache_k, cache_v, cache_kidx, state_rwkv, state_rwkv_shift,
              state_ffn_conv, page_table, meta_tokens, norm1_g, w_in, rwkv_mu, rwkv_w0, rwkv_w_w2,
              rwkv_a0, rwkv_w_a2, rwkv_w_g2, rwkv_k_k, rwkv_k_a, rwkv_r_k, rwkv_lnx_g, rwkv_lnx_b,
              w_br_rwkv, w_br_attn, w_out, norm2_g, w_up, conv_w, conv_b, w_down, final_norm_g):
    B = x_prompt.shape[0]
    meta = jnp.broadcast_to(meta_tokens[None].astype(x_prompt.dtype), (B, N_META, x_prompt.shape[2]))
    xp = jnp.concatenate([meta, x_prompt], axis=1)
    T = xp.shape[1]
    xs = x_sample
    past = page_table.shape[1] * PAGE_SIZE
    pos_p = jnp.arange(T)
    pos_s = past + jnp.arange(xs.shape[1])
    n_top_p = min(TOPK_MAX, x_prompt.shape[1] // 4)
    new_p, new_s = [], []
    for l in range(DEPTH):
        lp = {
            'norm1_g': norm1_g[l], 'w_in': w_in[l], 'rwkv_mu': rwkv_mu[l], 'rwkv_w0': rwkv_w0[l],
            'rwkv_w_w2': rwkv_w_w2[l], 'rwkv_a0': rwkv_a0[l], 'rwkv_w_a2': rwkv_w_a2[l],
            'rwkv_w_g2': rwkv_w_g2[l], 'rwkv_k_k': rwkv_k_k[l], 'rwkv_k_a': rwkv_k_a[l],
            'rwkv_r_k': rwkv_r_k[l], 'rwkv_lnx_g': rwkv_lnx_g[l], 'rwkv_lnx_b': rwkv_lnx_b[l],
            'w_br_rwkv': w_br_rwkv[l], 'w_br_attn': w_br_attn[l], 'w_out': w_out[l],
            'norm2_g': norm2_g[l], 'w_up': w_up[l], 'conv_w': conv_w[l], 'conv_b': conv_b[l],
            'w_down': w_down[l],
        }
        xp, st_p = trunk_layer(
            xp, pos_p, functools.partial(dsa_prompt, n_top=n_top_p),
            jnp.zeros((B, P_R), xp.dtype), jnp.zeros((B, H_R, HD_R, HD_R), jnp.float32),
            jnp.zeros((B, CONV_W - 1, D_FF), xp.dtype), lp)
        xs, st_s = trunk_layer(
            xs, pos_s,
            functools.partial(dsa_sample, cache_k=cache_k[l], cache_v=cache_v[l],
                              cache_kidx=cache_kidx[l], page_table=page_table),
            state_rwkv_shift[l], state_rwkv[l], state_ffn_conv[l], lp)
        new_p.append(st_p)
        new_s.append(st_s)
    stk = lambda lst, i: jnp.stack([e[i] for e in lst], axis=0)
    y_prompt = rms_norm(xp, final_norm_g)[:, N_META:]
    y_sample = rms_norm(xs, final_norm_g)
    return (y_prompt, y_sample,
            stk(new_p, 0), stk(new_p, 1), stk(new_p, 2), stk(new_p, 3), stk(new_p, 4), stk(new_p, 5),
            stk(new_s, 0), stk(new_s, 1), stk(new_s, 2), stk(new_s, 3), stk(new_s, 4), stk(new_s, 5))
```

```python
import functools
import math

import jax
import jax.numpy as jnp
import numpy as np
from jax import lax
from jax.experimental import pallas as pl
from jax.experimental.pallas import tpu as pltpu

F32 = jnp.float32
BF16 = jnp.bfloat16
I32 = jnp.int32

LANES = 128
SUBLANES = 8
VMEM_LIMIT = 56 * 1024 * 1024

N_META = 16
HD = 64
PAIR = 2 * HD
H_R = 8
C_R = H_R * HD
D_W_LORA, D_A_LORA, D_G_LORA = 64, 64, 128
P_R = 3 * C_R + D_W_LORA + D_A_LORA + D_G_LORA
LNX_EPS = 64e-5
H_A, H_KV = 8, 4
C_A, C_KV = H_A * HD, H_KV * HD
H_IDX, D_IDX = 8, 64
TOPK_MAX = 256
ROPE_THETA = 500000.0
ROT = HD // 4
ROT_HALF = ROT // 2
RMS_EPS = 1e-6
CONV_W = 3
Q_TILE = 128
INT_MIN = -(2 ** 31)
NEG_BIG = -1e30

_GROUPS = (("pr", P_R), ("q", C_A), ("qi", H_IDX * D_IDX), ("kd", 2 * C_KV), ("vd", 2 * C_KV),
           ("k", C_KV), ("v", C_KV), ("kw", LANES), ("ki2", LANES), ("g", None))


def _group_offsets(d_model):
    offs, o = {}, 0
    for name, width in _GROUPS:
        width = 2 * d_model if width is None else width
        offs[name] = (o, o + width)
        o += width
    return offs, o


def _pack_w_in(w_in, d_model):
    o = P_R
    q = w_in[:, o:o + C_A]; o += C_A
    k = w_in[:, o:o + C_KV]; o += C_KV
    v = w_in[:, o:o + C_KV]; o += C_KV
    qi = w_in[:, o:o + H_IDX * D_IDX]; o += H_IDX * D_IDX
    wi = w_in[:, o:o + H_IDX]; o += H_IDX
    ki = w_in[:, o:o + D_IDX]; o += D_IDX
    g = w_in[:, o:o + 2 * d_model]
    dup = lambda t: jnp.concatenate([t[:, (n // 2) * HD:(n // 2 + 1) * HD] for n in range(2 * H_KV)], axis=1)
    kw = jnp.concatenate([ki, wi, jnp.zeros((w_in.shape[0], LANES - D_IDX - H_IDX), w_in.dtype)], axis=1)
    packed = jnp.concatenate([w_in[:, :P_R], q, qi, dup(k), dup(v), k, v, kw, jnp.concatenate([ki, ki], axis=1), g],
                             axis=1)
    return packed.astype(BF16)


def _rope_tables(pos):
    inv = ROPE_THETA ** (-jnp.arange(ROT_HALF, dtype=F32) / ROT_HALF)
    ang = pos.astype(F32)[:, None] * inv[None, :]
    cos, sin = jnp.cos(ang), jnp.sin(ang)
    n = pos.shape[0]
    one = jnp.ones((n, HD - ROT), F32)
    zero = jnp.zeros((n, HD - ROT_HALF), F32)
    c64 = jnp.concatenate([cos, cos, one], axis=1)
    s1_64 = jnp.concatenate([-sin, zero], axis=1)
    s2_64 = jnp.concatenate([jnp.zeros((n, ROT_HALF), F32), sin, jnp.zeros((n, HD - ROT), F32)], axis=1)
    wi_scale = jnp.full((n, H_IDX), (H_IDX ** -0.5) * (D_IDX ** -0.5), F32)
    hi_c = jnp.concatenate([wi_scale, jnp.ones((n, HD - H_IDX), F32)], axis=1)
    z64 = jnp.zeros((n, HD), F32)
    return jnp.concatenate([c64, c64, s1_64, s1_64, s2_64, s2_64,
                            c64, hi_c, s1_64, z64, s2_64, z64], axis=1)


def _const_spec(shape):
    nd = len(shape)
    return pl.BlockSpec(shape, lambda *_: (0,) * nd, pipeline_mode=pl.Buffered(1))


def _half_masks(rows):
    lane = lax.broadcasted_iota(I32, (rows, PAIR), 1)
    return lane < HD


def _rms(x, g):
    return x * lax.rsqrt(jnp.mean(x * x, axis=-1, keepdims=True) + RMS_EPS) * g


def _dot(a, b):
    return jnp.dot(a, b, preferred_element_type=F32)


def _dot_nt(a, b):
    return lax.dot_general(a, b, (((1,), (1,)), ((), ())), preferred_element_type=F32)


def _dot_tn(a, b):
    return lax.dot_general(a, b, (((0,), (0,)), ((), ())), preferred_element_type=F32)


def _rope(h, c, s1, s2):
    outs = []
    for j in range(h.shape[1] // LANES):
        hj = h[:, j * LANES:(j + 1) * LANES]
        outs.append(hj * c + pltpu.roll(hj, LANES - ROT_HALF, 1) * s1 + pltpu.roll(hj, ROT_HALF, 1) * s2)
    return outs[0] if len(outs) == 1 else jnp.concatenate(outs, axis=1)


def _proj_body(offs, x_ref, g1_ref, w_ref, tab_ref, pr_o, q_o, qi_o, kd_o, vd_o, k_o, v_o, kw_o, ki2_o, g_o):
    xn = _rms(x_ref[...], g1_ref[...]).astype(BF16)
    mm = lambda name: _dot(xn, w_ref[:, offs[name][0]:offs[name][1]])
    tab = tab_ref[...]
    c, s1, s2 = (tab[:, i * LANES:(i + 1) * LANES] for i in range(3))
    ck, s1k, s2k = (tab[:, i * LANES:(i + 1) * LANES] for i in range(3, 6))
    pr_o[...] = mm("pr")
    g_o[...] = mm("g")
    v_o[...] = mm("v")
    vd_o[...] = mm("vd").astype(BF16)
    q_o[...] = _rope(mm("q"), c, s1, s2).astype(BF16)
    qi_o[...] = _rope(mm("qi"), c, s1, s2).astype(BF16)
    kd_o[...] = _rope(mm("kd"), c, s1, s2).astype(BF16)
    k_o[...] = _rope(mm("k"), c, s1, s2)
    kw_o[...] = _rope(mm("kw"), ck, s1k, s2k)
    ki2_o[...] = _rope(mm("ki2"), c, s1, s2).astype(BF16)


def _project(x, g1, w_packed, tab, tm):
    n, d = x.shape
    offs, n_cols = _group_offsets(d)
    period_tiles = tab.shape[0] // tm
    row = lambda w: pl.BlockSpec((tm, w), lambda i: (i, 0))
    widths = [(name, hi - lo) for name, (lo, hi) in offs.items()]
    dtypes = {"pr": F32, "q": BF16, "qi": BF16, "kd": BF16, "vd": BF16, "k": F32, "v": F32, "kw": F32,
              "ki2": BF16, "g": F32}
    outs = pl.pallas_call(
        functools.partial(_proj_body, offs),
        grid=(n // tm,),
        in_specs=[row(d), _const_spec((1, d)), _const_spec((d, n_cols)),
                  pl.BlockSpec((tm, tab.shape[1]), lambda i: (i % period_tiles, 0))],
        out_specs=[row(w) for _, w in widths],
        out_shape=[jax.ShapeDtypeStruct((n, w), dtypes[name]) for name, w in widths],
        compiler_params=pltpu.CompilerParams(dimension_semantics=("arbitrary",), vmem_limit_bytes=VMEM_LIMIT),
        name="in_proj",
    )(x, g1, w_packed, tab)
    return dict(zip([name for name, _ in widths], outs))


def _head_sum(x, h0):
    outs = []
    for p in range(x.shape[1] // PAIR):
        xp = x[:, p * PAIR:(p + 1) * PAIR]
        s0 = jnp.sum(jnp.where(h0, xp, 0.0), axis=1, keepdims=True)
        s1 = jnp.sum(jnp.where(h0, 0.0, xp), axis=1, keepdims=True)
        outs.append(jnp.where(h0, s0, s1))
    return jnp.concatenate(outs, axis=1)


def _rwkv_body(t_real, chunk, pr_ref, sh_ref, st_ref, mu_ref, w0_ref, ww2_ref, a0_ref, wa2_ref, wg2_ref,
               kk_ref, ka_ref, rk_ref, lg_ref, lb_ref, y_ref, so_ref, s_scr, xs_scr):
    c = pl.program_id(1)
    n_chunks = pl.num_programs(1)
    hdr = SUBLANES

    @pl.when(c == 0)
    def _():
        s_scr[...] = st_ref[...]
        xs_scr[hdr - 1:hdr, :] = sh_ref[...]

    @pl.when(c > 0)
    def _():
        xs_scr[hdr - 1:hdr, :] = xs_scr[hdr + chunk - 1:hdr + chunk, :]

    x = pr_ref[...]
    xs_scr[hdr:hdr + chunk, :] = x
    prev = xs_scr[hdr - 1:hdr - 1 + chunk, :]
    pm = x + (prev - x) * mu_ref[...]

    r = pm[:, 0:C_R]
    k = pm[:, C_R:2 * C_R]
    v = pm[:, 2 * C_R:3 * C_R]
    o = 3 * C_R
    wd = pm[:, o:o + D_W_LORA]; o += D_W_LORA
    ad = pm[:, o:o + D_A_LORA]; o += D_A_LORA
    gd = pm[:, o:o + D_G_LORA]

    z = -(w0_ref[...] + _dot(jnp.tanh(wd).astype(BF16), ww2_ref[...]))
    softplus = jnp.maximum(z, 0.0) + jnp.log1p(jnp.exp(-jnp.abs(z)))
    logdec = -jnp.exp(-softplus - 0.5)
    gate = jax.nn.sigmoid(a0_ref[...] + _dot(ad.astype(BF16), wa2_ref[...]))
    g_out = _dot(jax.nn.sigmoid(gd).astype(BF16), wg2_ref[...])

    h0 = _half_masks(chunk)
    kk = k * kk_ref[...]
    kk = kk * lax.rsqrt(jnp.maximum(_head_sum(kk * kk, h0), 1e-24))
    k2 = k * (1.0 + (gate - 1.0) * ka_ref[...])

    t_idx = c * chunk + lax.broadcasted_iota(I32, (chunk, 1), 0)
    valid = t_idx < t_real
    kk = jnp.where(valid, kk, 0.0)
    k2m = jnp.where(valid, k2, 0.0)
    logdec = jnp.where(valid, logdec, 0.0)

    ri = lax.broadcasted_iota(I32, (chunk, chunk), 0)
    ci = lax.broadcasted_iota(I32, (chunk, chunk), 1)
    incl = ri >= ci
    strict = ri > ci
    eye = (ri == ci).astype(F32)
    cum = jnp.dot(incl.astype(F32), logdec, preferred_element_type=F32, precision=lax.Precision.HIGHEST)
    cum_last = cum[chunk - 1:chunk, :]
    b = kk * gate
    a_t = -kk * jnp.exp(cum - logdec)
    e_neg = jnp.exp(-cum)
    b_t = b * e_neg
    k_t = k2m * e_neg
    r_t = r * jnp.exp(cum)
    e_tail = jnp.exp(cum_last - cum)
    b_g = b * e_tail
    k_g = k2m * e_tail
    g_last = jnp.exp(cum_last)

    n_sq = max(int(math.log2(chunk)) - 1, 0)
    rd = lax.broadcasted_iota(I32, (PAIR, PAIR), 0)
    cd = lax.broadcasted_iota(I32, (PAIR, PAIR), 1)
    blockdiag = (rd < HD) == (cd < HD)

    ys = []
    for p in range(H_R // 2):
        sl = slice(p * PAIR, (p + 1) * PAIR)
        a_p, b_p, k_p, r_p, v_p = a_t[:, sl], b_t[:, sl], k_t[:, sl], r_t[:, sl], v[:, sl]
        bk = jnp.concatenate([b_p, k_p], axis=0).astype(BF16)
        v16 = v_p.astype(BF16)
        inv, l_ak, t_rb, t_rk = [], [], [], []
        for j in range(2):
            hm = h0 if j == 0 else jnp.logical_not(h0)
            ar = jnp.concatenate([jnp.where(hm, a_p, 0.0), jnp.where(hm, r_p, 0.0)], axis=0).astype(BF16)
            xx = _dot_nt(ar, bk)
            l_ab = jnp.where(strict, xx[0:chunk, 0:chunk], 0.0)
            l_ak.append(jnp.where(strict, xx[0:chunk, chunk:], 0.0).astype(BF16))
            t_rb.append(jnp.where(incl, xx[chunk:, 0:chunk], 0.0).astype(BF16))
            t_rk.append(jnp.where(incl, xx[chunk:, chunk:], 0.0).astype(BF16))
            acc = eye + l_ab
            lp = l_ab
            for _ in range(n_sq):
                lp16 = lp.astype(BF16)
                lp = _dot(lp16, lp16)
                acc = acc + _dot(acc.astype(BF16), lp.astype(BF16))
            inv.append(acc.astype(BF16))
        per_head = lambda mats, rhs: jnp.where(h0, _dot(mats[0], rhs), _dot(mats[1], rhs))
        a_hat = per_head(inv, a_p.astype(BF16))
        u0 = per_head(inv, per_head(l_ak, v16).astype(BF16))
        a_hat16, u016 = a_hat.astype(BF16), u0.astype(BF16)
        r_hat = r_p + per_head(t_rb, a_hat16)
        y0 = per_head(t_rb, u016) + per_head(t_rk, v16)
        bg16 = b_g[:, sl].astype(BF16)
        kg16 = k_g[:, sl].astype(BF16)
        trans = jnp.where(blockdiag, _dot_tn(a_hat16, bg16), 0.0)
        add = jnp.where(blockdiag, _dot_tn(u016, bg16) + _dot_tn(v16, kg16), 0.0)
        s_old = s_scr[p]
        s16 = s_old.astype(BF16)
        ys.append(_dot_nt(r_hat.astype(BF16), s16) + y0)
        s_scr[p] = s_old * g_last[:, sl] + _dot(s16, trans.astype(BF16)) + add
    y = jnp.concatenate(ys, axis=1)

    mean = _head_sum(y, h0) * (1.0 / HD)
    d = y - mean
    var = _head_sum(d * d, h0) * (1.0 / HD)
    yn = d * lax.rsqrt(var + LNX_EPS) * lg_ref[...] + lb_ref[...]
    yn = yn + _head_sum(r * k2 * rk_ref[...], h0) * v
    y_ref[...] = (yn * g_out).astype(y_ref.dtype)

    @pl.when(c == n_chunks - 1)
    def _():
        so_ref[...] = s_scr[...]


def _rwkv(pr, shift_prev, state_bd, lp, n_batch, t_pad, t_real, chunk, y_dtype):
    n_chunks = t_pad // chunk
    n_pairs = H_R // 2
    vec = lambda name, w: lp[name].reshape(1, w).astype(F32)
    params = [vec("rwkv_mu", P_R), vec("rwkv_w0", C_R), lp["rwkv_w_w2"].astype(BF16), vec("rwkv_a0", C_R),
              lp["rwkv_w_a2"].astype(BF16), lp["rwkv_w_g2"].astype(BF16), vec("rwkv_k_k", C_R),
              vec("rwkv_k_a", C_R), vec("rwkv_r_k", C_R), vec("rwkv_lnx_g", C_R), vec("rwkv_lnx_b", C_R)]
    y, s_out = pl.pallas_call(
        functools.partial(_rwkv_body, t_real, chunk),
        grid=(n_batch, n_chunks),
        in_specs=[pl.BlockSpec((chunk, P_R), lambda b, c: (b * n_chunks + c, 0)),
                  pl.BlockSpec((None, 1, P_R), lambda b, c: (b, 0, 0)),
                  pl.BlockSpec((None, n_pairs, PAIR, PAIR), lambda b, c: (b, 0, 0, 0))]
                 + [_const_spec(p.shape) for p in params],
        out_specs=[pl.BlockSpec((chunk, C_R), lambda b, c: (b * n_chunks + c, 0)),
                   pl.BlockSpec((None, n_pairs, PAIR, PAIR), lambda b, c: (b, 0, 0, 0))],
        out_shape=[jax.ShapeDtypeStruct((n_batch * t_pad, C_R), y_dtype),
                   jax.ShapeDtypeStruct((n_batch, n_pairs, PAIR, PAIR), F32)],
        scratch_shapes=[pltpu.VMEM((n_pairs, PAIR, PAIR), F32), pltpu.VMEM((SUBLANES + chunk, P_R), F32)],
        compiler_params=pltpu.CompilerParams(dimension_semantics=("arbitrary", "arbitrary"),
                                             vmem_limit_bytes=VMEM_LIMIT),
        name="rwkv7_chunked",
    )(pr, shift_prev.reshape(n_batch, 1, P_R), state_bd, *params)
    return y, s_out


def _state_to_blockdiag(s):
    b = s.shape[0]
    s = s.reshape(b, H_R // 2, 2, HD, HD)
    z = jnp.zeros_like(s[:, :, 0])
    top = jnp.concatenate([s[:, :, 0], z], axis=-1)
    bot = jnp.concatenate([z, s[:, :, 1]], axis=-1)
    return jnp.concatenate([top, bot], axis=-2)


def _state_from_blockdiag(s):
    b = s.shape[0]
    return jnp.stack([s[:, :, :HD, :HD], s[:, :, HD:, HD:]], axis=2).reshape(b, H_R, HD, HD)


def _ordered_key(score, visible):
    bits = pltpu.bitcast(score + 0.0, I32)
    key = jnp.where(bits < 0, bits ^ jnp.int32(0x7FFFFFFF), bits)
    return jnp.where(visible, key, jnp.int32(INT_MIN))


def _topk_bias(key_scr, bias_scr, n_top):
    rows, width = key_scr.shape
    k_f = jnp.float32(n_top)

    def bit_step(it, prefix):
        bit = jnp.left_shift(jnp.int32(1), jnp.int32(31) - it)
        trial = prefix | bit
        cnt = jnp.sum(jnp.where(key_scr[...] >= (trial ^ jnp.int32(INT_MIN)), 1.0, 0.0), axis=1, keepdims=True)
        return jnp.where(cnt >= k_f, trial, prefix)

    prefix = lax.fori_loop(0, 32, bit_step, jnp.zeros((rows, 1), I32))
    thr = prefix ^ jnp.int32(INT_MIN)
    key = key_scr[...]
    n_gt = jnp.sum(jnp.where(key > thr, 1.0, 0.0), axis=1, keepdims=True)
    n_eq = jnp.sum(jnp.where(key == thr, 1.0, 0.0), axis=1, keepdims=True)
    need = k_f - n_gt
    bias_scr[...] = jnp.where((key >= thr) & (key > jnp.int32(INT_MIN)), 0.0, -jnp.inf)
    ambiguous = (n_eq > need) & (thr > jnp.int32(INT_MIN))

    @pl.when(jnp.max(jnp.where(ambiguous, 1.0, 0.0)) > 0.0)
    def _():
        upper = (lax.broadcasted_iota(I32, (LANES, LANES), 0)
                 < lax.broadcasted_iota(I32, (LANES, LANES), 1)).astype(BF16)

        def block(kb, seen):
            lo = pl.multiple_of(kb * LANES, LANES)
            kblk = key_scr[:, pl.ds(lo, LANES)]
            eq = kblk == thr
            eq16 = jnp.where(eq, 1.0, 0.0).astype(BF16)
            rank = seen + _dot(eq16, upper)
            take = (kblk > thr) | (eq & (rank < need))
            bias_scr[:, pl.ds(lo, LANES)] = jnp.where(take & (kblk > jnp.int32(INT_MIN)), 0.0, -jnp.inf)
            return seen + jnp.sum(jnp.where(eq, 1.0, 0.0), axis=1, keepdims=True)

        lax.fori_loop(0, width // LANES, block, jnp.zeros((rows, 1), F32))


def _indexer_scores(qi_ref, ki2, wi, h0):
    rows = qi_ref.shape[0]
    score = None
    for p in range(H_IDX // 2):
        qp = qi_ref[:, p * PAIR:(p + 1) * PAIR]
        zero = jnp.zeros_like(qp)
        lhs = jnp.concatenate([jnp.where(h0, qp, zero), jnp.where(h0, zero, qp)], axis=0)
        sc = jnp.maximum(_dot_nt(lhs, ki2), 0.0)
        part = wi[:, 2 * p:2 * p + 1] * sc[0:rows] + wi[:, 2 * p + 1:2 * p + 2] * sc[rows:]
        score = part if score is None else score + part
    return score


def _dsa_prompt_body(n_top, q_ref, qi_ref, kw_ref, ki2_ref, kd_ref, vd_ref, o_ref, key_scr, bias_scr):
    tq = q_ref.shape[0]
    t_keys = ki2_ref.shape[0]
    i = pl.program_id(1)
    h0 = _half_masks(tq)
    q_pos = i * tq + lax.broadcasted_iota(I32, (tq, 1), 0)
    visible = lax.broadcasted_iota(I32, (tq, t_keys), 1) <= q_pos

    wi = kw_ref[:, D_IDX:D_IDX + H_IDX]
    score = _indexer_scores(qi_ref, ki2_ref[...], wi, h0)
    key_scr[...] = _ordered_key(score, visible)
    _topk_bias(key_scr, bias_scr, n_top)

    bias = bias_scr[...]
    bias2 = jnp.concatenate([bias, bias], axis=0)
    scale = jnp.asarray(HD ** -0.5, BF16)
    for n in range(H_KV):
        sl = slice(n * PAIR, (n + 1) * PAIR)
        qp = q_ref[:, sl] * scale
        zero = jnp.zeros_like(qp)
        lhs = jnp.concatenate([jnp.where(h0, qp, zero), jnp.where(h0, zero, qp)], axis=0)
        s = _dot_nt(lhs, kd_ref[:, sl]) + bias2
        m = jnp.max(s, axis=1, keepdims=True)
        p = jnp.exp(s - m)
        l = jnp.sum(p, axis=1, keepdims=True)
        o = _dot(p.astype(BF16), vd_ref[:, sl]) / l
        o_ref[:, sl] = jnp.where(h0, o[0:tq], o[tq:]).astype(o_ref.dtype)


def _dsa_prompt(proj, n_batch, t_pad, n_top):
    nq = t_pad // Q_TILE
    qrow = lambda w: pl.BlockSpec((Q_TILE, w), lambda b, i: (b * nq + i, 0))
    seq = lambda w: pl.BlockSpec((None, t_pad, w), lambda b, i: (b, 0, 0))
    return pl.pallas_call(
        functools.partial(_dsa_prompt_body, n_top),
        grid=(n_batch, nq),
        in_specs=[qrow(C_A), qrow(H_IDX * D_IDX), qrow(LANES), seq(LANES), seq(2 * C_KV), seq(2 * C_KV)],
        out_specs=qrow(C_A),
        out_shape=jax.ShapeDtypeStruct((n_batch * t_pad, C_A), BF16),
        scratch_shapes=[pltpu.VMEM((Q_TILE, t_pad), I32), pltpu.VMEM((Q_TILE, t_pad), F32)],
        compiler_params=pltpu.CompilerParams(dimension_semantics=("arbitrary", "arbitrary"),
                                             vmem_limit_bytes=VMEM_LIMIT),
        name="dsa_prompt",
    )(proj["q"], proj["qi"], proj["kw"], proj["ki2"], proj["kd"], proj["vd"])


def _ffn_body(tiles_per_seq, stride, last_tile, last_lo, x_ref, yr_ref, ya_ref, g_ref, cp_ref, wbr_ref, wba_ref,
              wo_ref, g2_ref, wup_ref, cw_ref, cb_ref, wdn_ref, gf_ref, y_ref, cl_ref, a_scr):
    i = pl.program_id(0)
    tm, d = x_ref.shape
    d_ff = cb_ref.shape[1]
    hdr = max(SUBLANES, 2 * stride)

    @pl.when(i % tiles_per_seq == 0)
    def _():
        a_scr[hdr - 2 * stride:hdr, :] = cp_ref[...]

    @pl.when(i % tiles_per_seq != 0)
    def _():
        a_scr[hdr - 2 * stride:hdr, :] = a_scr[hdr + tm - 2 * stride:hdr + tm, :]

    g = g_ref[...]
    merged = (jax.nn.sigmoid(g[:, :d]) * _dot(yr_ref[...].astype(BF16), wbr_ref[...])
              + jax.nn.sigmoid(g[:, d:]) * _dot(ya_ref[...].astype(BF16), wba_ref[...]))
    x1 = x_ref[...] + _dot(merged.astype(BF16), wo_ref[...])
    hn = _rms(x1, g2_ref[...]).astype(BF16)
    a_scr[hdr:hdr + tm, :] = _dot(hn, wup_ref[:, :d_ff])
    gate = _dot(hn, wup_ref[:, d_ff:])
    cw = cw_ref[...]
    conv = cb_ref[...] + a_scr[hdr - 2 * stride:hdr - 2 * stride + tm, :] * cw[0:1]
    conv = conv + a_scr[hdr - stride:hdr - stride + tm, :] * cw[1:2]
    conv = conv + a_scr[hdr:hdr + tm, :] * cw[2:3]
    act = 0.5 * conv * (1.0 + lax.erf(conv * (2.0 ** -0.5)))
    x2 = x1 + _dot((act * gate).astype(BF16), wdn_ref[...])
    y_ref[...] = _rms(x2, gf_ref[...])

    @pl.when(i % tiles_per_seq == last_tile)
    def _():
        cl_ref[...] = a_scr[hdr + last_lo:hdr + last_lo + 2 * stride, :]


def _merge_ffn(x, yr, ya, g, conv_prev, lp, gf, tm, tiles_per_seq, stride, t_real):
    n, d = x.shape
    d_ff = lp["conv_b"].shape[-1]
    n_seq = n // (tm * tiles_per_seq)
    first_last = (t_real - 2) * stride
    last_tile, last_lo = first_last // tm, first_last % tm
    hdr = max(SUBLANES, 2 * stride)
    row = lambda w: pl.BlockSpec((tm, w), lambda i: (i, 0))
    weights = [lp["w_br_rwkv"].astype(BF16), lp["w_br_attn"].astype(BF16), lp["w_out"].astype(BF16),
               lp["norm2_g"].reshape(1, d), lp["w_up"].astype(BF16), lp["conv_w"], lp["conv_b"].reshape(1, d_ff),
               lp["w_down"].astype(BF16), gf.reshape(1, d)]
    y, conv_last = pl.pallas_call(
        functools.partial(_ffn_body, tiles_per_seq, stride, last_tile, last_lo),
        grid=(n // tm,),
        in_specs=[row(d), row(C_R), row(C_A), row(2 * d),
                  pl.BlockSpec((None, 2 * stride, d_ff), lambda i: (i // tiles_per_seq, 0, 0))]
                 + [_const_spec(w.shape) for w in weights],
        out_specs=[row(d), pl.BlockSpec((None, 2 * stride, d_ff), lambda i: (i // tiles_per_seq, 0, 0))],
        out_shape=[jax.ShapeDtypeStruct((n, d), F32), jax.ShapeDtypeStruct((n_seq, 2 * stride, d_ff), F32)],
        scratch_shapes=[pltpu.VMEM((hdr + tm, d_ff), F32)],
        compiler_params=pltpu.CompilerParams(dimension_semantics=("arbitrary",), vmem_limit_bytes=VMEM_LIMIT),
        name="merge_convffn",
    )(x, yr, ya, g, conv_prev, *weights)
    return y, conv_last


def _page_copies(pt_ref, batch, first_page, n_pages, srcs, dsts, sems, slot):
    def copies(pg):
        page = pt_ref[batch, first_page + pg]
        return [pltpu.make_async_copy(src.at[page], dst.at[slot, pg], sems.at[slot, a])
                for a, (src, dst) in enumerate(zip(srcs, dsts))]

    def start():
        def one(pg, carry):
            for cp in copies(pg):
                cp.start()
            return carry
        lax.fori_loop(0, n_pages, one, 0)

    def wait():
        def one(pg, carry):
            for cp in copies(pg):
                cp.wait()
            return carry
        lax.fori_loop(0, n_pages, one, 0)

    return start, wait


def _sample_index_body(n_top, group_pages, pt_ref, qi_ref, w_ref, kin_ref, cache_ref, bias_ref,
                       kbuf, sems, key_scr, bias_scr):
    b = pl.program_id(0)
    nb = pl.num_programs(0)
    _, n_pages, page, _ = kbuf.shape
    n_keys = n_pages * page
    n_q = key_scr.shape[0]
    slot = b % 2
    fetch = lambda bb, sl: _page_copies(pt_ref, bb, 0, n_pages, [cache_ref], [kbuf], sems, sl)

    @pl.when(b == 0)
    def _():
        fetch(0, 0)[0]()

    @pl.when(b + 1 < nb)
    def _():
        fetch(b + 1, 1 - slot)[0]()

    fetch(b, slot)[1]()

    qi = qi_ref[...]
    w = w_ref[:, 0:1]

    def head_mix(sc):
        sc = jnp.maximum(sc, 0.0) * w
        out = sc[0:n_q]
        for h in range(1, H_IDX):
            out = out + sc[h * n_q:(h + 1) * n_q]
        return out

    gk = group_pages * page
    for g in range(n_pages // group_pages):
        ki = kbuf[slot, g * group_pages:(g + 1) * group_pages].reshape(gk, D_IDX).astype(BF16)
        score = head_mix(_dot_nt(qi, ki))
        key_scr[:, g * gk:(g + 1) * gk] = _ordered_key(score, jnp.full(score.shape, True))
    score_new = head_mix(_dot_nt(qi, kin_ref[...]))
    vis_new = lax.broadcasted_iota(I32, score_new.shape, 1) <= lax.broadcasted_iota(I32, score_new.shape, 0)
    key_scr[:, n_keys:] = _ordered_key(score_new, vis_new)
    _topk_bias(key_scr, bias_scr, n_top)
    bias_ref[...] = bias_scr[...]


def _sample_attend_body(group_pages, pt_ref, q_ref, bias_ref, kn_ref, vn_ref, ck_ref, cv_ref, o_ref,
                        kbuf, vbuf, sems, m_scr, l_scr, acc_scr):
    b = pl.program_id(0)
    g = pl.program_id(1)
    nb = pl.num_programs(0)
    ng = pl.num_programs(1)
    page = kbuf.shape[2]
    gk = group_pages * page
    step = b * ng + g
    slot = step % 2
    fetch = lambda bb, gg, sl: _page_copies(pt_ref, bb, gg * group_pages, group_pages, [ck_ref, cv_ref],
                                            [kbuf, vbuf], sems, sl)

    @pl.when(step == 0)
    def _():
        fetch(0, 0, 0)[0]()

    @pl.when(step + 1 < nb * ng)
    def _():
        wrap = g + 1 == ng
        fetch(jnp.where(wrap, b + 1, b), jnp.where(wrap, 0, g + 1), 1 - slot)[0]()

    fetch(b, g, slot)[1]()

    @pl.when(g == 0)
    def _():
        m_scr[...] = jnp.full(m_scr.shape, NEG_BIG, F32)
        l_scr[...] = jnp.zeros(l_scr.shape, F32)
        acc_scr[...] = jnp.zeros(acc_scr.shape, F32)

    q = q_ref[...] * jnp.asarray(HD ** -0.5, BF16)
    reps = q.shape[0] // bias_ref.shape[0]

    def update(k16, v16, bias):
        s = _dot_nt(q, k16) + jnp.concatenate([bias] * reps, axis=0)
        m_old = m_scr[...]
        m_new = jnp.maximum(m_old, jnp.max(s, axis=1, keepdims=True))
        alpha = jnp.exp(m_old - m_new)
        p = jnp.exp(s - m_new)
        l_scr[...] = alpha * l_scr[...] + jnp.sum(p, axis=1, keepdims=True)
        acc_scr[...] = alpha * acc_scr[...] + _dot(p.astype(BF16), v16)
        m_scr[...] = m_new

    k16 = kbuf[slot].reshape(gk, kbuf.shape[3]).astype(BF16)
    v16 = vbuf[slot].reshape(gk, vbuf.shape[3]).astype(BF16)
    update(k16, v16, bias_ref[:, pl.ds(pl.multiple_of(g * gk, LANES), gk)])

    @pl.when(g == ng - 1)
    def _():
        update(kn_ref[...], vn_ref[...], bias_ref[:, ng * gk:])
        o_ref[...] = acc_scr[...] / l_scr[...]


def _dsa_sample(proj, cache_k, cache_v, cache_kidx, page_table, n_batch, n_q, n_top):
    n_pages = page_table.shape[1]
    n_pool, page = cache_kidx.shape[0], cache_kidx.shape[1]
    n_keys = n_pages * page
    width = n_keys + LANES
    rows = H_A * n_q
    per_q = lambda a, w: a.reshape(n_batch, n_q, w)
    heads_first = lambda a: per_q(a, H_A * HD).reshape(n_batch, n_q, H_A, HD).transpose(0, 2, 1, 3)
    pad_keys = lambda a: jnp.pad(a, ((0, 0), (0, LANES - n_q), (0, 0))).astype(BF16)

    qi = heads_first(proj["qi"]).reshape(n_batch, rows, D_IDX)
    wi = per_q(proj["kw"], LANES)[:, :, D_IDX:D_IDX + H_IDX].transpose(0, 2, 1).reshape(n_batch, rows, 1)
    wi = jnp.broadcast_to(wi, (n_batch, rows, LANES))
    ki_new = pad_keys(per_q(proj["kw"], LANES)[:, :, :D_IDX])
    idx_pages = 16
    bias = pl.pallas_call(
        functools.partial(_sample_index_body, n_top, idx_pages),
        grid_spec=pltpu.PrefetchScalarGridSpec(
            num_scalar_prefetch=1, grid=(n_batch,),
            in_specs=[pl.BlockSpec((None, rows, D_IDX), lambda b, pt: (b, 0, 0)),
                      pl.BlockSpec((None, rows, LANES), lambda b, pt: (b, 0, 0)),
                      pl.BlockSpec((None, LANES, D_IDX), lambda b, pt: (b, 0, 0)),
                      pl.BlockSpec(memory_space=pl.ANY)],
            out_specs=pl.BlockSpec((None, n_q, width), lambda b, pt: (b, 0, 0)),
            scratch_shapes=[pltpu.VMEM((2, n_pages, page, D_IDX), F32), pltpu.SemaphoreType.DMA((2, 1)),
                            pltpu.VMEM((n_q, width), I32), pltpu.VMEM((n_q, width), F32)]),
        out_shape=jax.ShapeDtypeStruct((n_batch, n_q, width), F32),
        compiler_params=pltpu.CompilerParams(dimension_semantics=("arbitrary",), vmem_limit_bytes=VMEM_LIMIT),
        name="dsa_sample_index",
    )(page_table, qi, wi, ki_new, cache_kidx)

    kv_of_head = (jnp.arange(H_A)[:, None] // (H_A // H_KV) == jnp.arange(H_KV)[None, :])
    q_bd = jnp.einsum("bhqd,hn->bhqnd", heads_first(proj["q"]), kv_of_head.astype(BF16)).reshape(n_batch, rows, C_KV)
    att_pages = 16
    o = pl.pallas_call(
        functools.partial(_sample_attend_body, att_pages),
        grid_spec=pltpu.PrefetchScalarGridSpec(
            num_scalar_prefetch=1, grid=(n_batch, n_pages // att_pages),
            in_specs=[pl.BlockSpec((None, rows, C_KV), lambda b, g, pt: (b, 0, 0)),
                      pl.BlockSpec((None, n_q, width), lambda b, g, pt: (b, 0, 0)),
                      pl.BlockSpec((None, LANES, C_KV), lambda b, g, pt: (b, 0, 0)),
                      pl.BlockSpec((None, LANES, C_KV), lambda b, g, pt: (b, 0, 0)),
                      pl.BlockSpec(memory_space=pl.ANY), pl.BlockSpec(memory_space=pl.ANY)],
            out_specs=pl.BlockSpec((None, rows, C_KV), lambda b, g, pt: (b, 0, 0)),
            scratch_shapes=[pltpu.VMEM((2, att_pages, page, C_KV), F32), pltpu.VMEM((2, att_pages, page, C_KV), F32),
                            pltpu.SemaphoreType.DMA((2, 2)), pltpu.VMEM((rows, 1), F32), pltpu.VMEM((rows, 1), F32),
                            pltpu.VMEM((rows, C_KV), F32)]),
        out_shape=jax.ShapeDtypeStruct((n_batch, rows, C_KV), F32),
        compiler_params=pltpu.CompilerParams(dimension_semantics=("arbitrary", "arbitrary"),
                                             vmem_limit_bytes=VMEM_LIMIT),
        name="dsa_sample_attend",
    )(page_table, q_bd, bias, pad_keys(per_q(proj["k"], C_KV)), pad_keys(per_q(proj["v"], C_KV)),
      cache_k.reshape(n_pool, page, C_KV), cache_v.reshape(n_pool, page, C_KV))
    o = o.reshape(n_batch, H_A, n_q, H_KV, HD)
    return jnp.einsum("bhqnd,hn->bqhd", o, kv_of_head.astype(F32)).reshape(n_batch * n_q, C_A)


def _round_up(x, m):
    return -(-x // m) * m


def kernel(x_prompt, x_sample, cache_k, cache_v, cache_kidx, state_rwkv, state_rwkv_shift, state_ffn_conv, page_table, meta_tokens, norm1_g, w_in, rwkv_mu, rwkv_w0, rwkv_w_w2, rwkv_a0, rwkv_w_a2, rwkv_w_g2, rwkv_k_k, rwkv_k_a, rwkv_r_k, rwkv_lnx_g, rwkv_lnx_b, w_br_rwkv, w_br_attn, w_out, norm2_g, w_up, conv_w, conv_b, w_down, final_norm_g):
    assert w_in.shape[0] == 1, "single-layer model"
    n_b, seq, d = x_prompt.shape
    n_s, n_q, _ = x_sample.shape
    d_ff = conv_b.shape[-1]
    lp = {"rwkv_mu": rwkv_mu[0], "rwkv_w0": rwkv_w0[0], "rwkv_w_w2": rwkv_w_w2[0], "rwkv_a0": rwkv_a0[0],
          "rwkv_w_a2": rwkv_w_a2[0], "rwkv_w_g2": rwkv_w_g2[0], "rwkv_k_k": rwkv_k_k[0], "rwkv_k_a": rwkv_k_a[0],
          "rwkv_r_k": rwkv_r_k[0], "rwkv_lnx_g": rwkv_lnx_g[0], "rwkv_lnx_b": rwkv_lnx_b[0],
          "w_br_rwkv": w_br_rwkv[0], "w_br_attn": w_br_attn[0], "w_out": w_out[0], "norm2_g": norm2_g[0],
          "w_up": w_up[0], "conv_w": conv_w[0], "conv_b": conv_b[0], "w_down": w_down[0]}
    w_packed = _pack_w_in(w_in[0], d)
    g1 = norm1_g[0].reshape(1, d)

    t_real = seq + N_META
    t_pad = _round_up(t_real, Q_TILE)
    tiles_per_seq = 8
    tm = t_pad // tiles_per_seq
    chunk = 64
    meta = jnp.broadcast_to(meta_tokens[None].astype(x_prompt.dtype), (n_b, N_META, d))
    xp = jnp.concatenate([meta, x_prompt, jnp.zeros((n_b, t_pad - t_real, d), x_prompt.dtype)], axis=1)
    xp = xp.reshape(n_b * t_pad, d)
    proj = _project(xp, g1, w_packed, _rope_tables(jnp.arange(t_pad)), tm)
    yr, s_fin = _rwkv(proj["pr"], jnp.zeros((n_b, P_R), F32), jnp.zeros((n_b, H_R // 2, PAIR, PAIR), F32), lp,
                      n_b, t_pad, t_real, chunk, BF16)
    seq3 = lambda a: a.reshape(n_b, t_pad, a.shape[-1])
    ya = _dsa_prompt({k_: (seq3(v_) if k_ in ("ki2", "kd", "vd") else v_) for k_, v_ in proj.items()},
                     n_b, t_pad, min(TOPK_MAX, seq // 4))
    y_p, conv_p = _merge_ffn(xp, yr, ya, proj["g"], jnp.zeros((n_b, CONV_W - 1, d_ff), F32), lp, final_norm_g,
                             tm, tiles_per_seq, 1, t_real)
    out_p = (seq3(y_p)[:, N_META:t_real],
             seq3(proj["k"])[:, :t_real].reshape(1, n_b, t_real, H_KV, HD),
             seq3(proj["v"])[:, :t_real].reshape(1, n_b, t_real, H_KV, HD),
             seq3(proj["kw"])[:, :t_real, :D_IDX][None],
             _state_from_blockdiag(s_fin)[None],
             seq3(proj["pr"])[:, t_real - 1][None],
             conv_p[None])

    past = page_table.shape[1] * cache_kidx.shape[2]
    xs = x_sample.reshape(n_s * n_q, d)
    tab_s = jnp.tile(_rope_tables(past + jnp.arange(n_q)), (n_s, 1))
    proj_s = _project(xs, g1, w_packed, tab_s, n_s * n_q)
    yr_s, s_fin_s = _rwkv(proj_s["pr"], state_rwkv_shift[0], _state_to_blockdiag(state_rwkv[0]), lp,
                          n_s, n_q, n_q, n_q, F32)
    ya_s = _dsa_sample(proj_s, cache_k[0], cache_v[0], cache_kidx[0], page_table, n_s, n_q,
                       min(TOPK_MAX, (past + n_q) // 4))
    time_major = lambda a: a.reshape(n_s, n_q, a.shape[-1]).transpose(1, 0, 2).reshape(n_q * n_s, a.shape[-1])
    conv_prev_s = state_ffn_conv[0].transpose(1, 0, 2).reshape(1, (CONV_W - 1) * n_s, d_ff)
    y_s, conv_s = _merge_ffn(time_major(xs), time_major(yr_s), time_major(ya_s), time_major(proj_s["g"]),
                             conv_prev_s, lp, final_norm_g, n_s * n_q, 1, n_s, n_q)
    per_q = lambda a: a.reshape(n_s, n_q, a.shape[-1])
    out_s = (y_s.reshape(n_q, n_s, d).transpose(1, 0, 2),
             per_q(proj_s["k"]).reshape(1, n_s, n_q, H_KV, HD),
             per_q(proj_s["v"]).reshape(1, n_s, n_q, H_KV, HD),
             per_q(proj_s["kw"])[:, :, :D_IDX][None],
             _state_from_blockdiag(s_fin_s)[None],
             per_q(proj_s["pr"])[:, n_q - 1][None],
             conv_s.reshape(CONV_W - 1, n_s, d_ff).transpose(1, 0, 2)[None])
    return (out_p[0], out_s[0]) + out_p[1:] + out_s[1:]
```

```python
import functools
import math

import jax
import jax.numpy as jnp
import numpy as np
from jax import lax
from jax.experimental import pallas as pl
from jax.experimental.pallas import tpu as pltpu

F32 = jnp.float32
BF16 = jnp.bfloat16
I32 = jnp.int32

LANES = 128
SUBLANES = 8
VMEM_LIMIT = 56 * 1024 * 1024

N_META = 16
HD = 64
PAIR = 2 * HD
H_R = 8
C_R = H_R * HD
D_W_LORA, D_A_LORA, D_G_LORA = 64, 64, 128
P_R = 3 * C_R + D_W_LORA + D_A_LORA + D_G_LORA
LNX_EPS = 64e-5
H_A, H_KV = 8, 4
C_A, C_KV = H_A * HD, H_KV * HD
H_IDX, D_IDX = 8, 64
TOPK_MAX = 256
ROPE_THETA = 500000.0
ROT = HD // 4
ROT_HALF = ROT // 2
RMS_EPS = 1e-6
CONV_W = 3
Q_TILE = 128
INT_MIN = -(2 ** 31)
NEG_BIG = -1e30

_GROUPS = (("pr", P_R), ("q", C_A), ("qi", H_IDX * D_IDX), ("kd", 2 * C_KV), ("vd", 2 * C_KV),
           ("k", C_KV), ("v", C_KV), ("kw", LANES), ("ki2", LANES), ("g", None))


def _group_offsets(d_model):
    offs, o = {}, 0
    for name, width in _GROUPS:
        width = 2 * d_model if width is None else width
        offs[name] = (o, o + width)
        o += width
    return offs, o


def _pack_w_in(w_in, d_model):
    o = P_R
    q = w_in[:, o:o + C_A]; o += C_A
    k = w_in[:, o:o + C_KV]; o += C_KV
    v = w_in[:, o:o + C_KV]; o += C_KV
    qi = w_in[:, o:o + H_IDX * D_IDX]; o += H_IDX * D_IDX
    wi = w_in[:, o:o + H_IDX]; o += H_IDX
    ki = w_in[:, o:o + D_IDX]; o += D_IDX
    g = w_in[:, o:o + 2 * d_model]
    dup = lambda t: jnp.concatenate([t[:, (n // 2) * HD:(n // 2 + 1) * HD] for n in range(2 * H_KV)], axis=1)
    kw = jnp.concatenate([ki, wi, jnp.zeros((w_in.shape[0], LANES - D_IDX - H_IDX), w_in.dtype)], axis=1)
    packed = jnp.concatenate([w_in[:, :P_R], q, qi, dup(k), dup(v), k, v, kw, jnp.concatenate([ki, ki], axis=1), g],
                             axis=1)
    return packed.astype(BF16)


def _rope_tables(pos):
    inv = ROPE_THETA ** (-jnp.arange(ROT_HALF, dtype=F32) / ROT_HALF)
    ang = pos.astype(F32)[:, None] * inv[None, :]
    cos, sin = jnp.cos(ang), jnp.sin(ang)
    n = pos.shape[0]
    one = jnp.ones((n, HD - ROT), F32)
    zero = jnp.zeros((n, HD - ROT_HALF), F32)
    c64 = jnp.concatenate([cos, cos, one], axis=1)
    s1_64 = jnp.concatenate([-sin, zero], axis=1)
    s2_64 = jnp.concatenate([jnp.zeros((n, ROT_HALF), F32), sin, jnp.zeros((n, HD - ROT), F32)], axis=1)
    wi_scale = jnp.full((n, H_IDX), (H_IDX ** -0.5) * (D_IDX ** -0.5), F32)
    hi_c = jnp.concatenate([wi_scale, jnp.ones((n, HD - H_IDX), F32)], axis=1)
    z64 = jnp.zeros((n, HD), F32)
    return jnp.concatenate([c64, c64, s1_64, s1_64, s2_64, s2_64,
                            c64, hi_c, s1_64, z64, s2_64, z64], axis=1)


def _const_spec(shape):
    nd = len(shape)
    return pl.BlockSpec(shape, lambda *_: (0,) * nd, pipeline_mode=pl.Buffered(1))


def _half_masks(rows):
    lane = lax.broadcasted_iota(I32, (rows, PAIR), 1)
    return lane < HD


def _rms(x, g):
    return x * lax.rsqrt(jnp.mean(x * x, axis=-1, keepdims=True) + RMS_EPS) * g


def _dot(a, b):
    return jnp.dot(a, b, preferred_element_type=F32)


def _dot_nt(a, b):
    return lax.dot_general(a, b, (((1,), (1,)), ((), ())), preferred_element_type=F32)


def _dot_tn(a, b):
    return lax.dot_general(a, b, (((0,), (0,)), ((), ())), preferred_element_type=F32)


def _rope(h, c, s1, s2):
    outs = []
    for j in range(h.shape[1] // LANES):
        hj = h[:, j * LANES:(j + 1) * LANES]
        outs.append(hj * c + pltpu.roll(hj, LANES - ROT_HALF, 1) * s1 + pltpu.roll(hj, ROT_HALF, 1) * s2)
    return outs[0] if len(outs) == 1 else jnp.concatenate(outs, axis=1)


def _proj_body(offs, x_ref, g1_ref, w_ref, tab_ref, pr_o, q_o, qi_o, kd_o, vd_o, k_o, v_o, kw_o, ki2_o, g_o):
    xn = _rms(x_ref[...], g1_ref[...]).astype(BF16)
    mm = lambda name: _dot(xn, w_ref[:, offs[name][0]:offs[name][1]])
    tab = tab_ref[...]
    c, s1, s2 = (tab[:, i * LANES:(i + 1) * LANES] for i in range(3))
    ck, s1k, s2k = (tab[:, i * LANES:(i + 1) * LANES] for i in range(3, 6))
    pr_o[...] = mm("pr")
    g_o[...] = mm("g")
    v_o[...] = mm("v")
    vd_o[...] = mm("vd").astype(BF16)
    q_o[...] = _rope(mm("q"), c, s1, s2).astype(BF16)
    qi_o[...] = _rope(mm("qi"), c, s1, s2).astype(BF16)
    kd_o[...] = _rope(mm("kd"), c, s1, s2).astype(BF16)
    k_o[...] = _rope(mm("k"), c, s1, s2)
    kw_o[...] = _rope(mm("kw"), ck, s1k, s2k)
    ki2_o[...] = _rope(mm("ki2"), c, s1, s2).astype(BF16)


def _project(x, g1, w_packed, tab, tm):
    n, d = x.shape
    offs, n_cols = _group_offsets(d)
    period_tiles = tab.shape[0] // tm
    row = lambda w: pl.BlockSpec((tm, w), lambda i: (i, 0))
    widths = [(name, hi - lo) for name, (lo, hi) in offs.items()]
    dtypes = {"pr": F32, "q": BF16, "qi": BF16, "kd": BF16, "vd": BF16, "k": F32, "v": F32, "kw": F32,
              "ki2": BF16, "g": F32}
    outs = pl.pallas_call(
        functools.partial(_proj_body, offs),
        grid=(n // tm,),
        in_specs=[row(d), _const_spec((1, d)), _const_spec((d, n_cols)),
                  pl.BlockSpec((tm, tab.shape[1]), lambda i: (i % period_tiles, 0))],
        out_specs=[row(w) for _, w in widths],
        out_shape=[jax.ShapeDtypeStruct((n, w), dtypes[name]) for name, w in widths],
        compiler_params=pltpu.CompilerParams(dimension_semantics=("arbitrary",), vmem_limit_bytes=VMEM_LIMIT),
        name="in_proj",
    )(x, g1, w_packed, tab)
    return dict(zip([name for name, _ in widths], outs))


def _head_sum(x, h0):
    outs = []
    for p in range(x.shape[1] // PAIR):
        xp = x[:, p * PAIR:(p + 1) * PAIR]
        s0 = jnp.sum(jnp.where(h0, xp, 0.0), axis=1, keepdims=True)
        s1 = jnp.sum(jnp.where(h0, 0.0, xp), axis=1, keepdims=True)
        outs.append(jnp.where(h0, s0, s1))
    return jnp.concatenate(outs, axis=1)


def _rwkv_body(t_real, chunk, pr_ref, sh_ref, st_ref, mu_ref, w0_ref, ww2_ref, a0_ref, wa2_ref, wg2_ref,
               kk_ref, ka_ref, rk_ref, lg_ref, lb_ref, y_ref, so_ref, s_scr, xs_scr):
    c = pl.program_id(1)
    n_chunks = pl.num_programs(1)
    hdr = SUBLANES

    @pl.when(c == 0)
    def _():
        s_scr[...] = st_ref[...]
        xs_scr[hdr - 1:hdr, :] = sh_ref[...]

    @pl.when(c > 0)
    def _():
        xs_scr[hdr - 1:hdr, :] = xs_scr[hdr + chunk - 1:hdr + chunk, :]

    x = pr_ref[...]
    xs_scr[hdr:hdr + chunk, :] = x
    prev = xs_scr[hdr - 1:hdr - 1 + chunk, :]
    pm = x + (prev - x) * mu_ref[...]

    r = pm[:, 0:C_R]
    k = pm[:, C_R:2 * C_R]
    v = pm[:, 2 * C_R:3 * C_R]
    o = 3 * C_R
    wd = pm[:, o:o + D_W_LORA]; o += D_W_LORA
    ad = pm[:, o:o + D_A_LORA]; o += D_A_LORA
    gd = pm[:, o:o + D_G_LORA]

    z = -(w0_ref[...] + _dot(jnp.tanh(wd).astype(BF16), ww2_ref[...]))
    softplus = jnp.maximum(z, 0.0) + jnp.log1p(jnp.exp(-jnp.abs(z)))
    logdec = -jnp.exp(-softplus - 0.5)
    gate = jax.nn.sigmoid(a0_ref[...] + _dot(ad.astype(BF16), wa2_ref[...]))
    g_out = _dot(jax.nn.sigmoid(gd).astype(BF16), wg2_ref[...])

    h0 = _half_masks(chunk)
    kk = k * kk_ref[...]
    kk = kk * lax.rsqrt(jnp.maximum(_head_sum(kk * kk, h0), 1e-24))
    k2 = k * (1.0 + (gate - 1.0) * ka_ref[...])

    t_idx = c * chunk + lax.broadcasted_iota(I32, (chunk, 1), 0)
    valid = t_idx < t_real
    kk = jnp.where(valid, kk, 0.0)
    k2m = jnp.where(valid, k2, 0.0)
    logdec = jnp.where(valid, logdec, 0.0)

    ri = lax.broadcasted_iota(I32, (chunk, chunk), 0)
    ci = lax.broadcasted_iota(I32, (chunk, chunk), 1)
    tri16 = jnp.where(ri >= ci, 1.0, 0.0).astype(BF16)
    ld_hi = logdec.astype(BF16)
    ld_r = logdec - ld_hi.astype(F32)
    ld_mid = ld_r.astype(BF16)
    ld_lo = (ld_r - ld_mid.astype(F32)).astype(BF16)
    cum = _dot(tri16, ld_hi) + _dot(tri16, ld_mid) + _dot(tri16, ld_lo)
    cum_last = cum[chunk - 1:chunk, :]
    b = kk * gate
    a_t = -kk * jnp.exp(cum - logdec)
    e_neg = jnp.exp(-cum)
    b_t = b * e_neg
    k_t = k2m * e_neg
    r_t = r * jnp.exp(cum)
    e_tail = jnp.exp(cum_last - cum)
    b_g = b * e_tail
    k_g = k2m * e_tail
    g_last = jnp.exp(cum_last)

    c2 = 2 * chunk
    pairs = range(H_R // 2)
    h0s = _half_masks(chunk)
    split = lambda x: jnp.concatenate([jnp.where(h0s, x, 0.0), jnp.where(h0s, 0.0, x)], axis=0)
    split16 = lambda x: split(x).astype(BF16)
    cols = lambda x, p: x[:, p * PAIR:(p + 1) * PAIR]
    rb = lax.broadcasted_iota(I32, (c2, c2), 0)
    cb = lax.broadcasted_iota(I32, (c2, c2), 1)
    same = (rb < chunk) == (cb < chunk)
    strict = same & (rb > cb)
    incl = same & (rb >= cb)
    eye = (rb == cb).astype(F32)
    n_sq = max(int(math.log2(chunk)) - 1, 0)

    a_s = [split16(cols(a_t, p)) for p in pairs]
    r_f = [split(cols(r_t, p)) for p in pairs]
    r_s = [r_f[p].astype(BF16) for p in pairs]
    v_s = [split16(cols(v, p)) for p in pairs]
    bg_s = [split16(cols(b_g, p)) for p in pairs]
    kg_s = [split16(cols(k_g, p)) for p in pairs]
    xx = [_dot_nt(jnp.concatenate([a_s[p], r_s[p]], axis=0),
                  jnp.concatenate([split16(cols(b_t, p)), split16(cols(k_t, p))], axis=0)) for p in pairs]
    l_ab = [jnp.where(strict, xx[p][0:c2, 0:c2], 0.0) for p in pairs]
    l_ak = [jnp.where(strict, xx[p][0:c2, c2:], 0.0).astype(BF16) for p in pairs]
    t_rb = [jnp.where(incl, xx[p][c2:, 0:c2], 0.0).astype(BF16) for p in pairs]
    t_rk = [jnp.where(incl, xx[p][c2:, c2:], 0.0).astype(BF16) for p in pairs]
    inv = [eye + l_ab[p] for p in pairs]
    lp = l_ab
    for _ in range(n_sq):
        lp16 = [lp[p].astype(BF16) for p in pairs]
        lp = [_dot(lp16[p], lp16[p]) for p in pairs]
        inv = [inv[p] + _dot(inv[p].astype(BF16), lp[p].astype(BF16)) for p in pairs]
    inv = [inv[p].astype(BF16) for p in pairs]
    a_hat = [_dot(inv[p], a_s[p]).astype(BF16) for p in pairs]
    w_s = [_dot(l_ak[p], v_s[p]).astype(BF16) for p in pairs]
    u0 = [_dot(inv[p], w_s[p]).astype(BF16) for p in pairs]
    r_hat = [(r_f[p] + _dot(t_rb[p], a_hat[p])).astype(BF16) for p in pairs]
    y0 = [_dot(t_rb[p], u0[p]) + _dot(t_rk[p], v_s[p]) for p in pairs]
    trans = [_dot_tn(a_hat[p], bg_s[p]).astype(BF16) for p in pairs]
    add = [_dot_tn(u0[p], bg_s[p]) + _dot_tn(v_s[p], kg_s[p]) for p in pairs]
    ys = []
    for p in pairs:
        s_old = s_scr[p]
        s16 = s_old.astype(BF16)
        y_split = _dot_nt(r_hat[p], s16) + y0[p]
        ys.append(y_split[0:chunk] + y_split[chunk:])
        s_scr[p] = s_old * cols(g_last, p) + _dot(s16, trans[p]) + add[p]
    y = jnp.concatenate(ys, axis=1)

    mean = _head_sum(y, h0) * (1.0 / HD)
    d = y - mean
    var = _head_sum(d * d, h0) * (1.0 / HD)
    yn = d * lax.rsqrt(var + LNX_EPS) * lg_ref[...] + lb_ref[...]
    yn = yn + _head_sum(r * k2 * rk_ref[...], h0) * v
    y_ref[...] = (yn * g_out).astype(y_ref.dtype)

    @pl.when(c == n_chunks - 1)
    def _():
        so_ref[...] = s_scr[...]


def _rwkv(pr, shift_prev, state_bd, lp, n_batch, t_pad, t_real, chunk, y_dtype):
    n_chunks = t_pad // chunk
    n_pairs = H_R // 2
    vec = lambda name, w: lp[name].reshape(1, w).astype(F32)
    params = [vec("rwkv_mu", P_R), vec("rwkv_w0", C_R), lp["rwkv_w_w2"].astype(BF16), vec("rwkv_a0", C_R),
              lp["rwkv_w_a2"].astype(BF16), lp["rwkv_w_g2"].astype(BF16), vec("rwkv_k_k", C_R),
              vec("rwkv_k_a", C_R), vec("rwkv_r_k", C_R), vec("rwkv_lnx_g", C_R), vec("rwkv_lnx_b", C_R)]
    y, s_out = pl.pallas_call(
        functools.partial(_rwkv_body, t_real, chunk),
        grid=(n_batch, n_chunks),
        in_specs=[pl.BlockSpec((chunk, P_R), lambda b, c: (b * n_chunks + c, 0)),
                  pl.BlockSpec((None, 1, P_R), lambda b, c: (b, 0, 0)),
                  pl.BlockSpec((None, n_pairs, PAIR, PAIR), lambda b, c: (b, 0, 0, 0))]
                 + [_const_spec(p.shape) for p in params],
        out_specs=[pl.BlockSpec((chunk, C_R), lambda b, c: (b * n_chunks + c, 0)),
                   pl.BlockSpec((None, n_pairs, PAIR, PAIR), lambda b, c: (b, 0, 0, 0))],
        out_shape=[jax.ShapeDtypeStruct((n_batch * t_pad, C_R), y_dtype),
                   jax.ShapeDtypeStruct((n_batch, n_pairs, PAIR, PAIR), F32)],
        scratch_shapes=[pltpu.VMEM((n_pairs, PAIR, PAIR), F32), pltpu.VMEM((SUBLANES + chunk, P_R), F32)],
        compiler_params=pltpu.CompilerParams(dimension_semantics=("arbitrary", "arbitrary"),
                                             vmem_limit_bytes=VMEM_LIMIT),
        name="rwkv7_chunked",
    )(pr, shift_prev.reshape(n_batch, 1, P_R), state_bd, *params)
    return y, s_out


def _state_to_blockdiag(s):
    b = s.shape[0]
    s = s.reshape(b, H_R // 2, 2, HD, HD)
    z = jnp.zeros_like(s[:, :, 0])
    top = jnp.concatenate([s[:, :, 0], z], axis=-1)
    bot = jnp.concatenate([z, s[:, :, 1]], axis=-1)
    return jnp.concatenate([top, bot], axis=-2)


def _state_from_blockdiag(s):
    b = s.shape[0]
    return jnp.stack([s[:, :, :HD, :HD], s[:, :, HD:, HD:]], axis=2).reshape(b, H_R, HD, HD)


def _ordered_key(score, visible):
    bits = pltpu.bitcast(score + 0.0, I32)
    key = jnp.where(bits < 0, bits ^ jnp.int32(0x7FFFFFFF), bits)
    return jnp.where(visible, key, jnp.int32(INT_MIN))


def _topk_bias(key_scr, bias_scr, n_top):
    rows, width = key_scr.shape
    k_f = jnp.float32(n_top)

    def bit_step(it, prefix):
        bit = jnp.left_shift(jnp.int32(1), jnp.int32(31) - it)
        trial = prefix | bit
        cnt = jnp.sum(jnp.where(key_scr[...] >= (trial ^ jnp.int32(INT_MIN)), 1.0, 0.0), axis=1, keepdims=True)
        return jnp.where(cnt >= k_f, trial, prefix)

    prefix = lax.fori_loop(0, 32, bit_step, jnp.zeros((rows, 1), I32))
    thr = prefix ^ jnp.int32(INT_MIN)
    key = key_scr[...]
    n_gt = jnp.sum(jnp.where(key > thr, 1.0, 0.0), axis=1, keepdims=True)
    n_eq = jnp.sum(jnp.where(key == thr, 1.0, 0.0), axis=1, keepdims=True)
    need = k_f - n_gt
    bias_scr[...] = jnp.where((key >= thr) & (key > jnp.int32(INT_MIN)), 0.0, -jnp.inf)
    ambiguous = (n_eq > need) & (thr > jnp.int32(INT_MIN))

    @pl.when(jnp.max(jnp.where(ambiguous, 1.0, 0.0)) > 0.0)
    def _():
        upper = (lax.broadcasted_iota(I32, (LANES, LANES), 0)
                 < lax.broadcasted_iota(I32, (LANES, LANES), 1)).astype(BF16)

        def block(kb, seen):
            lo = pl.multiple_of(kb * LANES, LANES)
            kblk = key_scr[:, pl.ds(lo, LANES)]
            eq = kblk == thr
            eq16 = jnp.where(eq, 1.0, 0.0).astype(BF16)
            rank = seen + _dot(eq16, upper)
            take = (kblk > thr) | (eq & (rank < need))
            bias_scr[:, pl.ds(lo, LANES)] = jnp.where(take & (kblk > jnp.int32(INT_MIN)), 0.0, -jnp.inf)
            return seen + jnp.sum(jnp.where(eq, 1.0, 0.0), axis=1, keepdims=True)

        lax.fori_loop(0, width // LANES, block, jnp.zeros((rows, 1), F32))


def _indexer_scores(qi_ref, ki2, wi, h0):
    rows = qi_ref.shape[0]
    score = None
    for p in range(H_IDX // 2):
        qp = qi_ref[:, p * PAIR:(p + 1) * PAIR]
        zero = jnp.zeros_like(qp)
        lhs = jnp.concatenate([jnp.where(h0, qp, zero), jnp.where(h0, zero, qp)], axis=0)
        sc = jnp.maximum(_dot_nt(lhs, ki2), 0.0)
        part = wi[:, 2 * p:2 * p + 1] * sc[0:rows] + wi[:, 2 * p + 1:2 * p + 2] * sc[rows:]
        score = part if score is None else score + part
    return score


def _dsa_prompt_tile(n_top, t_keys, i, q_ref, qi_ref, kw_ref, ki2_ref, kd_ref, vd_ref, o_ref, key_scr, bias_scr):
    tq = q_ref.shape[0]
    h0 = _half_masks(tq)
    q_pos = i * tq + lax.broadcasted_iota(I32, (tq, 1), 0)
    visible = lax.broadcasted_iota(I32, (tq, t_keys), 1) <= q_pos
    key_scr = key_scr.at[:, 0:t_keys]
    bias_scr = bias_scr.at[:, 0:t_keys]

    wi = kw_ref[:, D_IDX:D_IDX + H_IDX]
    score = _indexer_scores(qi_ref, ki2_ref[0:t_keys, :], wi, h0)
    key_scr[...] = _ordered_key(score, visible)
    _topk_bias(key_scr, bias_scr, n_top)

    bias = bias_scr[...]
    bias2 = jnp.concatenate([bias, bias], axis=0)
    scale = jnp.asarray(HD ** -0.5, BF16)
    for n in range(H_KV):
        sl = slice(n * PAIR, (n + 1) * PAIR)
        qp = q_ref[:, sl] * scale
        zero = jnp.zeros_like(qp)
        lhs = jnp.concatenate([jnp.where(h0, qp, zero), jnp.where(h0, zero, qp)], axis=0)
        s = _dot_nt(lhs, kd_ref[0:t_keys, sl]) + bias2
        m = jnp.max(s, axis=1, keepdims=True)
        p = jnp.exp(s - m)
        l = jnp.sum(p, axis=1, keepdims=True)
        o = _dot(p.astype(BF16), vd_ref[0:t_keys, sl]) / l
        o_ref[:, sl] = jnp.where(h0, o[0:tq], o[tq:]).astype(o_ref.dtype)


def _dsa_prompt_body(n_top, n_buckets, *refs):
    tq = refs[0].shape[0]
    nq = pl.num_programs(1)
    n_tiles = refs[3].shape[0] // tq
    i = pl.program_id(1)
    edges = [round(j * n_tiles / n_buckets) for j in range(n_buckets + 1)]
    for lo, hi in zip(edges[:-1], edges[1:]):
        if hi > lo:
            pl.when((i >= lo) & (i < hi))(functools.partial(_dsa_prompt_tile, n_top, hi * tq, i, *refs))


def _dsa_prompt(proj, n_batch, t_pad, n_top):
    nq = t_pad // Q_TILE
    qrow = lambda w: pl.BlockSpec((Q_TILE, w), lambda b, i: (b * nq + i, 0))
    seq = lambda w: pl.BlockSpec((None, t_pad, w), lambda b, i: (b, 0, 0))
    return pl.pallas_call(
        functools.partial(_dsa_prompt_body, n_top, 4),
        grid=(n_batch, nq),
        in_specs=[qrow(C_A), qrow(H_IDX * D_IDX), qrow(LANES), seq(LANES), seq(2 * C_KV), seq(2 * C_KV)],
        out_specs=qrow(C_A),
        out_shape=jax.ShapeDtypeStruct((n_batch * t_pad, C_A), BF16),
        scratch_shapes=[pltpu.VMEM((Q_TILE, t_pad), I32), pltpu.VMEM((Q_TILE, t_pad), F32)],
        compiler_params=pltpu.CompilerParams(dimension_semantics=("arbitrary", "arbitrary"),
                                             vmem_limit_bytes=VMEM_LIMIT),
        name="dsa_prompt",
    )(proj["q"], proj["qi"], proj["kw"], proj["ki2"], proj["kd"], proj["vd"])


def _ffn_body(tiles_per_seq, stride, last_tile, last_lo, x_ref, yr_ref, ya_ref, g_ref, cp_ref, wbr_ref, wba_ref,
              wo_ref, g2_ref, wup_ref, cw_ref, cb_ref, wdn_ref, gf_ref, y_ref, cl_ref, a_scr):
    i = pl.program_id(0)
    tm, d = x_ref.shape
    d_ff = cb_ref.shape[1]
    hdr = max(SUBLANES, 2 * stride)

    @pl.when(i % tiles_per_seq == 0)
    def _():
        a_scr[hdr - 2 * stride:hdr, :] = cp_ref[...]

    @pl.when(i % tiles_per_seq != 0)
    def _():
        a_scr[hdr - 2 * stride:hdr, :] = a_scr[hdr + tm - 2 * stride:hdr + tm, :]

    g = g_ref[...]
    merged = (jax.nn.sigmoid(g[:, :d]) * _dot(yr_ref[...].astype(BF16), wbr_ref[...])
              + jax.nn.sigmoid(g[:, d:]) * _dot(ya_ref[...].astype(BF16), wba_ref[...]))
    x1 = x_ref[...] + _dot(merged.astype(BF16), wo_ref[...])
    hn = _rms(x1, g2_ref[...]).astype(BF16)
    a_scr[hdr:hdr + tm, :] = _dot(hn, wup_ref[:, :d_ff])
    gate = _dot(hn, wup_ref[:, d_ff:])
    cw = cw_ref[...]
    conv = cb_ref[...] + a_scr[hdr - 2 * stride:hdr - 2 * stride + tm, :] * cw[0:1]
    conv = conv + a_scr[hdr - stride:hdr - stride + tm, :] * cw[1:2]
    conv = conv + a_scr[hdr:hdr + tm, :] * cw[2:3]
    act = 0.5 * conv * (1.0 + lax.erf(conv * (2.0 ** -0.5)))
    x2 = x1 + _dot((act * gate).astype(BF16), wdn_ref[...])
    y_ref[...] = _rms(x2, gf_ref[...])

    @pl.when(i % tiles_per_seq == last_tile)
    def _():
        cl_ref[...] = a_scr[hdr + last_lo:hdr + last_lo + 2 * stride, :]


def _merge_ffn(x, yr, ya, g, conv_prev, lp, gf, tm, tiles_per_seq, stride, t_real):
    n, d = x.shape
    d_ff = lp["conv_b"].shape[-1]
    n_seq = n // (tm * tiles_per_seq)
    first_last = (t_real - 2) * stride
    last_tile, last_lo = first_last // tm, first_last % tm
    hdr = max(SUBLANES, 2 * stride)
    row = lambda w: pl.BlockSpec((tm, w), lambda i: (i, 0))
    weights = [lp["w_br_rwkv"].astype(BF16), lp["w_br_attn"].astype(BF16), lp["w_out"].astype(BF16),
               lp["norm2_g"].reshape(1, d), lp["w_up"].astype(BF16), lp["conv_w"], lp["conv_b"].reshape(1, d_ff),
               lp["w_down"].astype(BF16), gf.reshape(1, d)]
    y, conv_last = pl.pallas_call(
        functools.partial(_ffn_body, tiles_per_seq, stride, last_tile, last_lo),
        grid=(n // tm,),
        in_specs=[row(d), row(C_R), row(C_A), row(2 * d),
                  pl.BlockSpec((None, 2 * stride, d_ff), lambda i: (i // tiles_per_seq, 0, 0))]
                 + [_const_spec(w.shape) for w in weights],
        out_specs=[row(d), pl.BlockSpec((None, 2 * stride, d_ff), lambda i: (i // tiles_per_seq, 0, 0))],
        out_shape=[jax.ShapeDtypeStruct((n, d), F32), jax.ShapeDtypeStruct((n_seq, 2 * stride, d_ff), F32)],
        scratch_shapes=[pltpu.VMEM((hdr + tm, d_ff), F32)],
        compiler_params=pltpu.CompilerParams(dimension_semantics=("arbitrary",), vmem_limit_bytes=VMEM_LIMIT),
        name="merge_convffn",
    )(x, yr, ya, g, conv_prev, *weights)
    return y, conv_last


def _page_copies(pt_ref, batch, first_page, n_pages, srcs, dsts, sems, slot):
    def copies(pg):
        page = pt_ref[batch, first_page + pg]
        rows = srcs[0].shape[-1]
        lanes = pl.ds(pl.multiple_of(pg * rows, rows), rows)
        window = lambda dst: dst.at[(slot,) + (slice(None),) * (len(dst.shape) - 2) + (lanes,)]
        return [pltpu.make_async_copy(src.at[page], window(dst), sems.at[slot, a])
                for a, (src, dst) in enumerate(zip(srcs, dsts))]

    def start():
        def one(pg, carry):
            for cp in copies(pg):
                cp.start()
            return carry
        lax.fori_loop(0, n_pages, one, 0)

    def wait():
        def one(pg, carry):
            for cp in copies(pg):
                cp.wait()
            return carry
        lax.fori_loop(0, n_pages, one, 0)

    return start, wait


def _sample_index_body(n_top, group_pages, pt_ref, qi_ref, w_ref, kin_ref, cache_ref, bias_ref,
                       kbuf, sems, key_scr, bias_scr):
    b = pl.program_id(0)
    nb = pl.num_programs(0)
    page = cache_ref.shape[-1]
    n_keys = kbuf.shape[-1]
    n_pages = n_keys // page
    n_q = key_scr.shape[0]
    slot = b % 2
    fetch = lambda bb, sl: _page_copies(pt_ref, bb, 0, n_pages, [cache_ref], [kbuf], sems, sl)

    @pl.when(b == 0)
    def _():
        fetch(0, 0)[0]()

    @pl.when(b + 1 < nb)
    def _():
        fetch(b + 1, 1 - slot)[0]()

    fetch(b, slot)[1]()

    qi = qi_ref[...]
    w = w_ref[:, 0:1]

    def head_mix(sc):
        sc = jnp.maximum(sc, 0.0) * w
        out = sc[0:n_q]
        for h in range(1, H_IDX):
            out = out + sc[h * n_q:(h + 1) * n_q]
        return out

    gk = group_pages * page
    for g in range(n_pages // group_pages):
        ki_t = kbuf[slot, :, g * gk:(g + 1) * gk].astype(BF16)
        score = head_mix(_dot(qi, ki_t))
        key_scr[:, g * gk:(g + 1) * gk] = _ordered_key(score, jnp.full(score.shape, True))
    score_new = head_mix(_dot_nt(qi, kin_ref[...]))
    vis_new = lax.broadcasted_iota(I32, score_new.shape, 1) <= lax.broadcasted_iota(I32, score_new.shape, 0)
    key_scr[:, n_keys:] = _ordered_key(score_new, vis_new)
    _topk_bias(key_scr, bias_scr, n_top)
    bias_ref[...] = bias_scr[...]


def _sample_attend_body(group_pages, pt_ref, q_ref, bias_ref, kn_ref, vn_ref, ck_ref, cv_ref, o_ref,
                        kbuf, vbuf, sems, m_scr, l_scr, acc_scr):
    b = pl.program_id(0)
    g = pl.program_id(1)
    nb = pl.num_programs(0)
    ng = pl.num_programs(1)
    gk = kbuf.shape[-1]
    step = b * ng + g
    slot = step % 2
    fetch = lambda bb, gg, sl: _page_copies(pt_ref, bb, gg * group_pages, group_pages, [ck_ref, cv_ref],
                                            [kbuf, vbuf], sems, sl)

    @pl.when(step == 0)
    def _():
        fetch(0, 0, 0)[0]()

    @pl.when(step + 1 < nb * ng)
    def _():
        wrap = g + 1 == ng
        fetch(jnp.where(wrap, b + 1, b), jnp.where(wrap, 0, g + 1), 1 - slot)[0]()

    fetch(b, g, slot)[1]()

    @pl.when(g == 0)
    def _():
        m_scr[...] = jnp.full(m_scr.shape, NEG_BIG, F32)
        l_scr[...] = jnp.zeros(l_scr.shape, F32)
        acc_scr[...] = jnp.zeros(acc_scr.shape, F32)

    q = q_ref[...] * jnp.asarray(HD ** -0.5, BF16)
    per_kv = q.shape[1]
    reps = per_kv // bias_ref.shape[0]

    def update(k_t, v_t, bias):
        bias_g = jnp.concatenate([bias] * reps, axis=0)
        s = jnp.concatenate([_dot(q[n], k_t[n]) + bias_g for n in range(H_KV)], axis=0)
        m_old = m_scr[...]
        m_new = jnp.maximum(m_old, jnp.max(s, axis=1, keepdims=True))
        alpha = jnp.exp(m_old - m_new)
        p = jnp.exp(s - m_new)
        l_scr[...] = alpha * l_scr[...] + jnp.sum(p, axis=1, keepdims=True)
        p16 = p.astype(BF16)
        pv = jnp.concatenate([_dot_nt(p16[n * per_kv:(n + 1) * per_kv], v_t[n]) for n in range(H_KV)], axis=0)
        acc_scr[...] = alpha * acc_scr[...] + pv
        m_scr[...] = m_new

    update(kbuf[slot].astype(BF16), vbuf[slot].astype(BF16),
           bias_ref[:, pl.ds(pl.multiple_of(g * gk, LANES), gk)])

    @pl.when(g == ng - 1)
    def _():
        update(kn_ref[...], vn_ref[...], bias_ref[:, ng * gk:])
        o_ref[...] = acc_scr[...] / l_scr[...]


def _dsa_sample(proj, cache_k, cache_v, cache_kidx, page_table, n_batch, n_q, n_top):
    n_pages = page_table.shape[1]
    n_pool, page = cache_kidx.shape[0], cache_kidx.shape[1]
    n_keys = n_pages * page
    width = n_keys + LANES
    rows = H_A * n_q
    per_q = lambda a, w: a.reshape(n_batch, n_q, w)
    heads_first = lambda a: per_q(a, H_A * HD).reshape(n_batch, n_q, H_A, HD).transpose(0, 2, 1, 3)
    pad_keys = lambda a: jnp.pad(a, ((0, 0), (0, LANES - n_q), (0, 0))).astype(BF16)

    qi = heads_first(proj["qi"]).reshape(n_batch, rows, D_IDX)
    wi = per_q(proj["kw"], LANES)[:, :, D_IDX:D_IDX + H_IDX].transpose(0, 2, 1).reshape(n_batch, rows, 1)
    wi = jnp.broadcast_to(wi, (n_batch, rows, LANES))
    ki_new = pad_keys(per_q(proj["kw"], LANES)[:, :, :D_IDX])
    idx_pages = 16
    bias = pl.pallas_call(
        functools.partial(_sample_index_body, n_top, idx_pages),
        grid_spec=pltpu.PrefetchScalarGridSpec(
            num_scalar_prefetch=1, grid=(n_batch,),
            in_specs=[pl.BlockSpec((None, rows, D_IDX), lambda b, pt: (b, 0, 0)),
                      pl.BlockSpec((None, rows, LANES), lambda b, pt: (b, 0, 0)),
                      pl.BlockSpec((None, LANES, D_IDX), lambda b, pt: (b, 0, 0)),
                      pl.BlockSpec(memory_space=pl.ANY)],
            out_specs=pl.BlockSpec((None, n_q, width), lambda b, pt: (b, 0, 0)),
            scratch_shapes=[pltpu.VMEM((2, D_IDX, n_keys), F32), pltpu.SemaphoreType.DMA((2, 1)),
                            pltpu.VMEM((n_q, width), I32), pltpu.VMEM((n_q, width), F32)]),
        out_shape=jax.ShapeDtypeStruct((n_batch, n_q, width), F32),
        compiler_params=pltpu.CompilerParams(dimension_semantics=("arbitrary",), vmem_limit_bytes=VMEM_LIMIT),
        name="dsa_sample_index",
    )(page_table, qi, wi, ki_new, cache_kidx.transpose(0, 2, 1))

    group = H_A // H_KV
    per_kv = group * n_q
    q_kv = per_q(proj["q"], C_A).reshape(n_batch, n_q, H_KV, group, HD).transpose(0, 2, 3, 1, 4)
    q_kv = q_kv.reshape(n_batch, H_KV, per_kv, HD)
    new_t = lambda a: pad_keys(per_q(a, C_KV)).reshape(n_batch, LANES, H_KV, HD).transpose(0, 2, 3, 1)
    att_pages = min(32, n_pages)
    kv_spec = pl.BlockSpec((None, H_KV, HD, LANES), lambda b, g, pt: (b, 0, 0, 0))
    o = pl.pallas_call(
        functools.partial(_sample_attend_body, att_pages),
        grid_spec=pltpu.PrefetchScalarGridSpec(
            num_scalar_prefetch=1, grid=(n_batch, n_pages // att_pages),
            in_specs=[pl.BlockSpec((None, H_KV, per_kv, HD), lambda b, g, pt: (b, 0, 0, 0)),
                      pl.BlockSpec((None, n_q, width), lambda b, g, pt: (b, 0, 0)),
                      kv_spec, kv_spec,
                      pl.BlockSpec(memory_space=pl.ANY), pl.BlockSpec(memory_space=pl.ANY)],
            out_specs=pl.BlockSpec((None, H_KV * per_kv, HD), lambda b, g, pt: (b, 0, 0)),
            scratch_shapes=[pltpu.VMEM((2, H_KV, HD, att_pages * page), F32),
                            pltpu.VMEM((2, H_KV, HD, att_pages * page), F32),
                            pltpu.SemaphoreType.DMA((2, 2)), pltpu.VMEM((rows, 1), F32), pltpu.VMEM((rows, 1), F32),
                            pltpu.VMEM((rows, HD), F32)]),
        out_shape=jax.ShapeDtypeStruct((n_batch, rows, HD), F32),
        compiler_params=pltpu.CompilerParams(dimension_semantics=("arbitrary", "arbitrary"),
                                             vmem_limit_bytes=VMEM_LIMIT),
        name="dsa_sample_attend",
    )(page_table, q_kv, bias, new_t(proj["k"]), new_t(proj["v"]),
      cache_k.transpose(0, 2, 3, 1), cache_v.transpose(0, 2, 3, 1))
    o = o.reshape(n_batch, H_KV, group, n_q, HD).transpose(0, 3, 1, 2, 4)
    return o.reshape(n_batch * n_q, C_A)


def _round_up(x, m):
    return -(-x // m) * m


def kernel(x_prompt, x_sample, cache_k, cache_v, cache_kidx, state_rwkv, state_rwkv_shift, state_ffn_conv, page_table, meta_tokens, norm1_g, w_in, rwkv_mu, rwkv_w0, rwkv_w_w2, rwkv_a0, rwkv_w_a2, rwkv_w_g2, rwkv_k_k, rwkv_k_a, rwkv_r_k, rwkv_lnx_g, rwkv_lnx_b, w_br_rwkv, w_br_attn, w_out, norm2_g, w_up, conv_w, conv_b, w_down, final_norm_g):
    assert w_in.shape[0] == 1, "single-layer model"
    n_b, seq, d = x_prompt.shape
    n_s, n_q, _ = x_sample.shape
    d_ff = conv_b.shape[-1]
    lp = {"rwkv_mu": rwkv_mu[0], "rwkv_w0": rwkv_w0[0], "rwkv_w_w2": rwkv_w_w2[0], "rwkv_a0": rwkv_a0[0],
          "rwkv_w_a2": rwkv_w_a2[0], "rwkv_w_g2": rwkv_w_g2[0], "rwkv_k_k": rwkv_k_k[0], "rwkv_k_a": rwkv_k_a[0],
          "rwkv_r_k": rwkv_r_k[0], "rwkv_lnx_g": rwkv_lnx_g[0], "rwkv_lnx_b": rwkv_lnx_b[0],
          "w_br_rwkv": w_br_rwkv[0], "w_br_attn": w_br_attn[0], "w_out": w_out[0], "norm2_g": norm2_g[0],
          "w_up": w_up[0], "conv_w": conv_w[0], "conv_b": conv_b[0], "w_down": w_down[0]}
    w_packed = _pack_w_in(w_in[0], d)
    g1 = norm1_g[0].reshape(1, d)

    t_real = seq + N_META
    t_pad = _round_up(t_real, Q_TILE)
    tiles_per_seq = 8
    tm = t_pad // tiles_per_seq
    chunk = 64
    meta = jnp.broadcast_to(meta_tokens[None].astype(x_prompt.dtype), (n_b, N_META, d))
    xp = jnp.concatenate([meta, x_prompt, jnp.zeros((n_b, t_pad - t_real, d), x_prompt.dtype)], axis=1)
    xp = xp.reshape(n_b * t_pad, d)
    proj = _project(xp, g1, w_packed, _rope_tables(jnp.arange(t_pad)), tm)
    yr, s_fin = _rwkv(proj["pr"], jnp.zeros((n_b, P_R), F32), jnp.zeros((n_b, H_R // 2, PAIR, PAIR), F32), lp,
                      n_b, t_pad, t_real, chunk, BF16)
    seq3 = lambda a: a.reshape(n_b, t_pad, a.shape[-1])
    ya = _dsa_prompt({k_: (seq3(v_) if k_ in ("ki2", "kd", "vd") else v_) for k_, v_ in proj.items()},
                     n_b, t_pad, min(TOPK_MAX, seq // 4))
    y_p, conv_p = _merge_ffn(xp, yr, ya, proj["g"], jnp.zeros((n_b, CONV_W - 1, d_ff), F32), lp, final_norm_g,
                             tm, tiles_per_seq, 1, t_real)
    out_p = (seq3(y_p)[:, N_META:t_real],
             seq3(proj["k"])[:, :t_real].reshape(1, n_b, t_real, H_KV, HD),
             seq3(proj["v"])[:, :t_real].reshape(1, n_b, t_real, H_KV, HD),
             seq3(proj["kw"])[:, :t_real, :D_IDX][None],
             _state_from_blockdiag(s_fin)[None],
             seq3(proj["pr"])[:, t_real - 1][None],
             conv_p[None])

    past = page_table.shape[1] * cache_kidx.shape[2]
    xs = x_sample.reshape(n_s * n_q, d)
    tab_s = jnp.tile(_rope_tables(past + jnp.arange(n_q)), (n_s, 1))
    proj_s = _project(xs, g1, w_packed, tab_s, n_s * n_q)
    yr_s, s_fin_s = _rwkv(proj_s["pr"], state_rwkv_shift[0], _state_to_blockdiag(state_rwkv[0]), lp,
                          n_s, n_q, n_q, n_q, F32)
    ya_s = _dsa_sample(proj_s, cache_k[0], cache_v[0], cache_kidx[0], page_table, n_s, n_q,
                       min(TOPK_MAX, (past + n_q) // 4))
    time_major = lambda a: a.reshape(n_s, n_q, a.shape[-1]).transpose(1, 0, 2).reshape(n_q * n_s, a.shape[-1])
    conv_prev_s = state_ffn_conv[0].transpose(1, 0, 2).reshape(1, (CONV_W - 1) * n_s, d_ff)
    y_s, conv_s = _merge_ffn(time_major(xs), time_major(yr_s), time_major(ya_s), time_major(proj_s["g"]),
                             conv_prev_s, lp, final_norm_g, n_s * n_q, 1, n_s, n_q)
    per_q = lambda a: a.reshape(n_s, n_q, a.shape[-1])
    out_s = (y_s.reshape(n_q, n_s, d).transpose(1, 0, 2),
             per_q(proj_s["k"]).reshape(1, n_s, n_q, H_KV, HD),
             per_q(proj_s["v"]).reshape(1, n_s, n_q, H_KV, HD),
             per_q(proj_s["kw"])[:, :, :D_IDX][None],
             _state_from_blockdiag(s_fin_s)[None],
             per_q(proj_s["pr"])[:, n_q - 1][None],
             conv_s.reshape(CONV_W - 1, n_s, d_ff).transpose(1, 0, 2)[None])
    return (out_p[0], out_s[0]) + out_p[1:] + out_s[1:]
```

```python
import functools
import math

import jax
import jax.numpy as jnp
import numpy as np
from jax import lax
from jax.experimental import pallas as pl
from jax.experimental.pallas import tpu as pltpu

F32 = jnp.float32
BF16 = jnp.bfloat16
I32 = jnp.int32

LANES = 128
SUBLANES = 8
VMEM_LIMIT = 56 * 1024 * 1024

N_META = 16
HD = 64
PAIR = 2 * HD
H_R = 8
C_R = H_R * HD
D_W_LORA, D_A_LORA, D_G_LORA = 64, 64, 128
P_R = 3 * C_R + D_W_LORA + D_A_LORA + D_G_LORA
LNX_EPS = 64e-5
H_A, H_KV = 8, 4
C_A, C_KV = H_A * HD, H_KV * HD
H_IDX, D_IDX = 8, 64
TOPK_MAX = 256
ROPE_THETA = 500000.0
ROT = HD // 4
ROT_HALF = ROT // 2
RMS_EPS = 1e-6
CONV_W = 3
Q_TILE = 128
INT_MIN = -(2 ** 31)
NEG_BIG = -1e30

_GROUPS = (("pr", P_R), ("q", C_A), ("qi", H_IDX * D_IDX), ("kd", 2 * C_KV), ("vd", 2 * C_KV),
           ("k", C_KV), ("v", C_KV), ("kw", LANES), ("ki2", LANES), ("g", None))


def _group_offsets(d_model):
    offs, o = {}, 0
    for name, width in _GROUPS:
        width = 2 * d_model if width is None else width
        offs[name] = (o, o + width)
        o += width
    return offs, o


def _pack_w_in(w_in, d_model):
    o = P_R
    q = w_in[:, o:o + C_A]; o += C_A
    k = w_in[:, o:o + C_KV]; o += C_KV
    v = w_in[:, o:o + C_KV]; o += C_KV
    qi = w_in[:, o:o + H_IDX * D_IDX]; o += H_IDX * D_IDX
    wi = w_in[:, o:o + H_IDX]; o += H_IDX
    ki = w_in[:, o:o + D_IDX]; o += D_IDX
    g = w_in[:, o:o + 2 * d_model]
    dup = lambda t: jnp.concatenate([t[:, (n // 2) * HD:(n // 2 + 1) * HD] for n in range(2 * H_KV)], axis=1)
    kw = jnp.concatenate([ki, wi, jnp.zeros((w_in.shape[0], LANES - D_IDX - H_IDX), w_in.dtype)], axis=1)
    packed = jnp.concatenate([w_in[:, :P_R], q, qi, dup(k), dup(v), k, v, kw, jnp.concatenate([ki, ki], axis=1), g],
                             axis=1)
    return packed.astype(BF16)


def _rope_tables(pos):
    inv = ROPE_THETA ** (-jnp.arange(ROT_HALF, dtype=F32) / ROT_HALF)
    ang = pos.astype(F32)[:, None] * inv[None, :]
    cos, sin = jnp.cos(ang), jnp.sin(ang)
    n = pos.shape[0]
    one = jnp.ones((n, HD - ROT), F32)
    zero = jnp.zeros((n, HD - ROT_HALF), F32)
    c64 = jnp.concatenate([cos, cos, one], axis=1)
    s1_64 = jnp.concatenate([-sin, zero], axis=1)
    s2_64 = jnp.concatenate([jnp.zeros((n, ROT_HALF), F32), sin, jnp.zeros((n, HD - ROT), F32)], axis=1)
    wi_scale = jnp.full((n, H_IDX), (H_IDX ** -0.5) * (D_IDX ** -0.5), F32)
    hi_c = jnp.concatenate([wi_scale, jnp.ones((n, HD - H_IDX), F32)], axis=1)
    z64 = jnp.zeros((n, HD), F32)
    return jnp.concatenate([c64, c64, s1_64, s1_64, s2_64, s2_64,
                            c64, hi_c, s1_64, z64, s2_64, z64], axis=1)


def _const_spec(shape):
    nd = len(shape)
    return pl.BlockSpec(shape, lambda *_: (0,) * nd, pipeline_mode=pl.Buffered(1))


def _half_masks(rows):
    lane = lax.broadcasted_iota(I32, (rows, PAIR), 1)
    return lane < HD


def _rms(x, g):
    return x * lax.rsqrt(jnp.mean(x * x, axis=-1, keepdims=True) + RMS_EPS) * g


def _dot(a, b):
    return jnp.dot(a, b, preferred_element_type=F32)


def _dot_nt(a, b):
    return lax.dot_general(a, b, (((1,), (1,)), ((), ())), preferred_element_type=F32)


def _dot_tn(a, b):
    return lax.dot_general(a, b, (((0,), (0,)), ((), ())), preferred_element_type=F32)


def _rope(h, c, s1, s2):
    outs = []
    for j in range(h.shape[1] // LANES):
        hj = h[:, j * LANES:(j + 1) * LANES]
        outs.append(hj * c + pltpu.roll(hj, LANES - ROT_HALF, 1) * s1 + pltpu.roll(hj, ROT_HALF, 1) * s2)
    return outs[0] if len(outs) == 1 else jnp.concatenate(outs, axis=1)


def _proj_body(offs, x_ref, g1_ref, w_ref, tab_ref, pr_o, q_o, qi_o, kd_o, vd_o, k_o, v_o, kw_o, ki2_o, g_o):
    xn = _rms(x_ref[...], g1_ref[...]).astype(BF16)
    mm = lambda name: _dot(xn, w_ref[:, offs[name][0]:offs[name][1]])
    tab = tab_ref[...]
    c, s1, s2 = (tab[:, i * LANES:(i + 1) * LANES] for i in range(3))
    ck, s1k, s2k = (tab[:, i * LANES:(i + 1) * LANES] for i in range(3, 6))
    pr_o[...] = mm("pr")
    g_o[...] = mm("g")
    v_o[...] = mm("v")
    vd_o[...] = mm("vd").astype(BF16)
    q_o[...] = _rope(mm("q"), c, s1, s2).astype(BF16)
    qi_o[...] = _rope(mm("qi"), c, s1, s2).astype(BF16)
    kd_o[...] = _rope(mm("kd"), c, s1, s2).astype(BF16)
    k_o[...] = _rope(mm("k"), c, s1, s2)
    kw_o[...] = _rope(mm("kw"), ck, s1k, s2k)
    ki2_o[...] = _rope(mm("ki2"), c, s1, s2).astype(BF16)


def _project(x, g1, w_packed, tab, tm):
    n, d = x.shape
    offs, n_cols = _group_offsets(d)
    period_tiles = tab.shape[0] // tm
    row = lambda w: pl.BlockSpec((tm, w), lambda i: (i, 0))
    widths = [(name, hi - lo) for name, (lo, hi) in offs.items()]
    dtypes = {"pr": F32, "q": BF16, "qi": BF16, "kd": BF16, "vd": BF16, "k": F32, "v": F32, "kw": F32,
              "ki2": BF16, "g": F32}
    outs = pl.pallas_call(
        functools.partial(_proj_body, offs),
        grid=(n // tm,),
        in_specs=[row(d), _const_spec((1, d)), _const_spec((d, n_cols)),
                  pl.BlockSpec((tm, tab.shape[1]), lambda i: (i % period_tiles, 0))],
        out_specs=[row(w) for _, w in widths],
        out_shape=[jax.ShapeDtypeStruct((n, w), dtypes[name]) for name, w in widths],
        compiler_params=pltpu.CompilerParams(dimension_semantics=("arbitrary",), vmem_limit_bytes=VMEM_LIMIT),
        name="in_proj",
    )(x, g1, w_packed, tab)
    return dict(zip([name for name, _ in widths], outs))


def _head_sum(x, h0):
    outs = []
    for p in range(x.shape[1] // PAIR):
        xp = x[:, p * PAIR:(p + 1) * PAIR]
        s0 = jnp.sum(jnp.where(h0, xp, 0.0), axis=1, keepdims=True)
        s1 = jnp.sum(jnp.where(h0, 0.0, xp), axis=1, keepdims=True)
        outs.append(jnp.where(h0, s0, s1))
    return jnp.concatenate(outs, axis=1)


def _rwkv_prep(chunk, valid, x, prev, mu_ref, w0_ref, ww2_ref, a0_ref, wa2_ref, wg2_ref, kk_ref, ka_ref):
    pm = x + (prev - x) * mu_ref[...]

    r = pm[:, 0:C_R]
    k = pm[:, C_R:2 * C_R]
    v = pm[:, 2 * C_R:3 * C_R]
    o = 3 * C_R
    wd = pm[:, o:o + D_W_LORA]; o += D_W_LORA
    ad = pm[:, o:o + D_A_LORA]; o += D_A_LORA
    gd = pm[:, o:o + D_G_LORA]

    z = -(w0_ref[...] + _dot(jnp.tanh(wd).astype(BF16), ww2_ref[...]))
    softplus = jnp.maximum(z, 0.0) + jnp.log1p(jnp.exp(-jnp.abs(z)))
    logdec = -jnp.exp(-softplus - 0.5)
    gate = jax.nn.sigmoid(a0_ref[...] + _dot(ad.astype(BF16), wa2_ref[...]))
    g_out = _dot(jax.nn.sigmoid(gd).astype(BF16), wg2_ref[...])

    h0 = _half_masks(chunk)
    kk = k * kk_ref[...]
    kk = kk * lax.rsqrt(jnp.maximum(_head_sum(kk * kk, h0), 1e-24))
    k2 = k * (1.0 + (gate - 1.0) * ka_ref[...])

    kk = jnp.where(valid, kk, 0.0)
    k2m = jnp.where(valid, k2, 0.0)
    logdec = jnp.where(valid, logdec, 0.0)

    ri = lax.broadcasted_iota(I32, (chunk, chunk), 0)
    ci = lax.broadcasted_iota(I32, (chunk, chunk), 1)
    tri16 = jnp.where(ri >= ci, 1.0, 0.0).astype(BF16)
    ld_hi = logdec.astype(BF16)
    ld_r = logdec - ld_hi.astype(F32)
    ld_mid = ld_r.astype(BF16)
    ld_lo = (ld_r - ld_mid.astype(F32)).astype(BF16)
    cum = _dot(tri16, ld_hi) + _dot(tri16, ld_mid) + _dot(tri16, ld_lo)
    cum_last = cum[chunk - 1:chunk, :]
    b = kk * gate
    a_t = -kk * jnp.exp(cum - logdec)
    e_neg = jnp.exp(-cum)
    b_t = b * e_neg
    k_t = k2m * e_neg
    r_t = r * jnp.exp(cum)
    e_tail = jnp.exp(cum_last - cum)
    b_g = b * e_tail
    k_g = k2m * e_tail
    g_last = jnp.exp(cum_last)
    return dict(r=r, k2=k2, v=v, g_out=g_out, a_t=a_t, b_t=b_t, k_t=k_t, r_t=r_t, b_g=b_g, k_g=k_g, g_last=g_last)


def _rwkv_body(t_real, chunk, pr_ref, sh_ref, st_ref, mu_ref, w0_ref, ww2_ref, a0_ref, wa2_ref, wg2_ref,
               kk_ref, ka_ref, rk_ref, lg_ref, lb_ref, y_ref, so_ref, s_scr, xs_scr):
    c = pl.program_id(1)
    n_chunks = pl.num_programs(1)
    n_seq = pr_ref.shape[0]
    hdr = SUBLANES

    @pl.when(c == 0)
    def _():
        s_scr[...] = st_ref[...]
        xs_scr[:, hdr - 1:hdr, :] = sh_ref[...]

    @pl.when(c > 0)
    def _():
        xs_scr[:, hdr - 1:hdr, :] = xs_scr[:, hdr + chunk - 1:hdr + chunk, :]

    valid = c * chunk + lax.broadcasted_iota(I32, (chunk, 1), 0) < t_real
    seqs = []
    for i in range(n_seq):
        x = pr_ref[i]
        xs_scr[i, hdr:hdr + chunk, :] = x
        prev = xs_scr[i, hdr - 1:hdr - 1 + chunk, :]
        seqs.append(_rwkv_prep(chunk, valid, x, prev, mu_ref, w0_ref, ww2_ref, a0_ref, wa2_ref, wg2_ref,
                               kk_ref, ka_ref))

    c2 = 2 * chunk
    units = [(i, p) for i in range(n_seq) for p in range(H_R // 2)]
    idx = range(len(units))
    h0 = _half_masks(chunk)
    split = lambda x: jnp.concatenate([jnp.where(h0, x, 0.0), jnp.where(h0, 0.0, x)], axis=0)
    split16 = lambda x: split(x).astype(BF16)
    cols = lambda x, p: x[:, p * PAIR:(p + 1) * PAIR]
    get = lambda name, u: cols(seqs[units[u][0]][name], units[u][1])
    rb = lax.broadcasted_iota(I32, (c2, c2), 0)
    cb = lax.broadcasted_iota(I32, (c2, c2), 1)
    same = (rb < chunk) == (cb < chunk)
    strict = same & (rb > cb)
    incl = same & (rb >= cb)
    eye = (rb == cb).astype(F32)
    n_sq = max(int(math.log2(chunk)) - 1, 0)

    a_s = [split16(get("a_t", u)) for u in idx]
    r_f = [split(get("r_t", u)) for u in idx]
    r_s = [r_f[u].astype(BF16) for u in idx]
    v_s = [split16(get("v", u)) for u in idx]
    bg_s = [split16(get("b_g", u)) for u in idx]
    kg_s = [split16(get("k_g", u)) for u in idx]
    xx = [_dot_nt(jnp.concatenate([a_s[u], r_s[u]], axis=0),
                  jnp.concatenate([split16(get("b_t", u)), split16(get("k_t", u))], axis=0)) for u in idx]
    l_ab = [jnp.where(strict, xx[u][0:c2, 0:c2], 0.0) for u in idx]
    l_ak = [jnp.where(strict, xx[u][0:c2, c2:], 0.0).astype(BF16) for u in idx]
    t_rb = [jnp.where(incl, xx[u][c2:, 0:c2], 0.0).astype(BF16) for u in idx]
    t_rk = [jnp.where(incl, xx[u][c2:, c2:], 0.0).astype(BF16) for u in idx]
    inv = [eye + l_ab[u] for u in idx]
    lp = l_ab
    for _ in range(n_sq):
        lp16 = [lp[u].astype(BF16) for u in idx]
        lp = [_dot(lp16[u], lp16[u]) for u in idx]
        inv = [inv[u] + _dot(inv[u].astype(BF16), lp[u].astype(BF16)) for u in idx]
    inv = [inv[u].astype(BF16) for u in idx]
    a_hat = [_dot(inv[u], a_s[u]).astype(BF16) for u in idx]
    w_s = [_dot(l_ak[u], v_s[u]).astype(BF16) for u in idx]
    u0 = [_dot(inv[u], w_s[u]).astype(BF16) for u in idx]
    r_hat = [(r_f[u] + _dot(t_rb[u], a_hat[u])).astype(BF16) for u in idx]
    y0 = [_dot(t_rb[u], u0[u]) + _dot(t_rk[u], v_s[u]) for u in idx]
    trans = [_dot_tn(a_hat[u], bg_s[u]).astype(BF16) for u in idx]
    add = [_dot_tn(u0[u], bg_s[u]) + _dot_tn(v_s[u], kg_s[u]) for u in idx]
    ys = [[] for _ in range(n_seq)]
    for u, (i, p) in enumerate(units):
        s_old = s_scr[i, p]
        s16 = s_old.astype(BF16)
        y_split = _dot_nt(r_hat[u], s16) + y0[u]
        ys[i].append(y_split[0:chunk] + y_split[chunk:])
        s_scr[i, p] = s_old * get("g_last", u) + _dot(s16, trans[u]) + add[u]

    for i in range(n_seq):
        y = jnp.concatenate(ys[i], axis=1)
        sq = seqs[i]
        mean = _head_sum(y, h0) * (1.0 / HD)
        d = y - mean
        var = _head_sum(d * d, h0) * (1.0 / HD)
        yn = d * lax.rsqrt(var + LNX_EPS) * lg_ref[...] + lb_ref[...]
        yn = yn + _head_sum(sq["r"] * sq["k2"] * rk_ref[...], h0) * sq["v"]
        y_ref[i] = (yn * sq["g_out"]).astype(y_ref.dtype)

    @pl.when(c == n_chunks - 1)
    def _():
        so_ref[...] = s_scr[...]


def _rwkv(pr, shift_prev, state_bd, lp, n_batch, t_pad, t_real, chunk, y_dtype):
    n_chunks = t_pad // chunk
    n_pairs = H_R // 2
    vec = lambda name, w: lp[name].reshape(1, w).astype(F32)
    params = [vec("rwkv_mu", P_R), vec("rwkv_w0", C_R), lp["rwkv_w_w2"].astype(BF16), vec("rwkv_a0", C_R),
              lp["rwkv_w_a2"].astype(BF16), lp["rwkv_w_g2"].astype(BF16), vec("rwkv_k_k", C_R),
              vec("rwkv_k_a", C_R), vec("rwkv_r_k", C_R), vec("rwkv_lnx_g", C_R), vec("rwkv_lnx_b", C_R)]
    n_seq = next(n for n in (4, 2, 1) if n_batch % n == 0)
    y, s_out = pl.pallas_call(
        functools.partial(_rwkv_body, t_real, chunk),
        grid=(n_batch // n_seq, n_chunks),
        in_specs=[pl.BlockSpec((n_seq, chunk, P_R), lambda b, c: (b, c, 0)),
                  pl.BlockSpec((n_seq, 1, P_R), lambda b, c: (b, 0, 0)),
                  pl.BlockSpec((n_seq, n_pairs, PAIR, PAIR), lambda b, c: (b, 0, 0, 0))]
                 + [_const_spec(p.shape) for p in params],
        out_specs=[pl.BlockSpec((n_seq, chunk, C_R), lambda b, c: (b, c, 0)),
                   pl.BlockSpec((n_seq, n_pairs, PAIR, PAIR), lambda b, c: (b, 0, 0, 0))],
        out_shape=[jax.ShapeDtypeStruct((n_batch, t_pad, C_R), y_dtype),
                   jax.ShapeDtypeStruct((n_batch, n_pairs, PAIR, PAIR), F32)],
        scratch_shapes=[pltpu.VMEM((n_seq, n_pairs, PAIR, PAIR), F32),
                        pltpu.VMEM((n_seq, SUBLANES + chunk, P_R), F32)],
        compiler_params=pltpu.CompilerParams(dimension_semantics=("arbitrary", "arbitrary"),
                                             vmem_limit_bytes=VMEM_LIMIT),
        name="rwkv7_chunked",
    )(pr.reshape(n_batch, t_pad, P_R), shift_prev.reshape(n_batch, 1, P_R), state_bd, *params)
    return y.reshape(n_batch * t_pad, C_R), s_out


def _state_to_blockdiag(s):
    b = s.shape[0]
    s = s.reshape(b, H_R // 2, 2, HD, HD)
    z = jnp.zeros_like(s[:, :, 0])
    top = jnp.concatenate([s[:, :, 0], z], axis=-1)
    bot = jnp.concatenate([z, s[:, :, 1]], axis=-1)
    return jnp.concatenate([top, bot], axis=-2)


def _state_from_blockdiag(s):
    b = s.shape[0]
    return jnp.stack([s[:, :, :HD, :HD], s[:, :, HD:, HD:]], axis=2).reshape(b, H_R, HD, HD)


def _ordered_key(score, visible):
    bits = pltpu.bitcast(score + 0.0, I32)
    key = jnp.where(bits < 0, bits ^ jnp.int32(0x7FFFFFFF), bits)
    return jnp.where(visible, key, jnp.int32(INT_MIN))


def _topk_bias(key_scr, bias_scr, n_top):
    rows, width = key_scr.shape
    k_f = jnp.float32(n_top)
    n_parts = max(1, min(width // LANES, (8 * SUBLANES) // rows))
    bounds = [round(j * (width // LANES) / n_parts) * LANES for j in range(n_parts + 1)]

    def count_ge(thr):
        parts = [jnp.sum(jnp.where(key_scr[:, lo:hi] >= thr, 1.0, 0.0), axis=1, keepdims=True)
                 for lo, hi in zip(bounds[:-1], bounds[1:])]
        return functools.reduce(lambda a, b: a + b, parts)

    def bit_step(it, carry):
        prefix, n_ge = carry
        bit = jnp.left_shift(jnp.int32(1), jnp.int32(31) - it)
        trial = prefix | bit
        cnt = count_ge(trial ^ jnp.int32(INT_MIN))
        take = cnt >= k_f
        return jnp.where(take, trial, prefix), jnp.where(take, cnt, n_ge)

    prefix, n_ge = lax.fori_loop(0, 32, bit_step,
                                 (jnp.zeros((rows, 1), I32), jnp.full((rows, 1), width, F32)), unroll=4)
    thr = prefix ^ jnp.int32(INT_MIN)
    bias_scr[...] = jnp.where(key_scr[...] >= jnp.maximum(thr, jnp.int32(INT_MIN + 1)), 0.0, -jnp.inf)
    ambiguous = (n_ge > k_f) & (thr > jnp.int32(INT_MIN))

    @pl.when(jnp.max(jnp.where(ambiguous, 1.0, 0.0)) > 0.0)
    def _():
        upper = (lax.broadcasted_iota(I32, (LANES, LANES), 0)
                 < lax.broadcasted_iota(I32, (LANES, LANES), 1)).astype(BF16)
        n_gt = jnp.sum(jnp.where(key_scr[...] > thr, 1.0, 0.0), axis=1, keepdims=True)
        need = k_f - n_gt

        def block(kb, seen):
            lo = pl.multiple_of(kb * LANES, LANES)
            kblk = key_scr[:, pl.ds(lo, LANES)]
            eq = kblk == thr
            eq16 = jnp.where(eq, 1.0, 0.0).astype(BF16)
            rank = seen + _dot(eq16, upper)
            take = (kblk > thr) | (eq & (rank < need))
            bias_scr[:, pl.ds(lo, LANES)] = jnp.where(take & (kblk > jnp.int32(INT_MIN)), 0.0, -jnp.inf)
            return seen + jnp.sum(jnp.where(eq, 1.0, 0.0), axis=1, keepdims=True)

        lax.fori_loop(0, width // LANES, block, jnp.zeros((rows, 1), F32))


def _indexer_scores(qi_ref, ki2, wi, h0):
    rows = qi_ref.shape[0]
    score = None
    for p in range(H_IDX // 2):
        qp = qi_ref[:, p * PAIR:(p + 1) * PAIR]
        zero = jnp.zeros_like(qp)
        lhs = jnp.concatenate([jnp.where(h0, qp, zero), jnp.where(h0, zero, qp)], axis=0)
        sc = jnp.maximum(_dot_nt(lhs, ki2), 0.0)
        part = wi[:, 2 * p:2 * p + 1] * sc[0:rows] + wi[:, 2 * p + 1:2 * p + 2] * sc[rows:]
        score = part if score is None else score + part
    return score


def _dsa_prompt_tile(n_top, t_keys, i, q_ref, qi_ref, kw_ref, ki2_ref, kd_ref, vd_ref, o_ref, key_scr, bias_scr):
    tq = q_ref.shape[0]
    h0 = _half_masks(tq)
    q_pos = i * tq + lax.broadcasted_iota(I32, (tq, 1), 0)
    visible = lax.broadcasted_iota(I32, (tq, t_keys), 1) <= q_pos
    key_scr = key_scr.at[:, 0:t_keys]
    bias_scr = bias_scr.at[:, 0:t_keys]

    wi = kw_ref[:, D_IDX:D_IDX + H_IDX]
    score = _indexer_scores(qi_ref, ki2_ref[0:t_keys, :], wi, h0)
    key_scr[...] = _ordered_key(score, visible)
    _topk_bias(key_scr, bias_scr, n_top)

    bias = bias_scr[...]
    bias2 = jnp.concatenate([bias, bias], axis=0)
    scale = jnp.asarray(HD ** -0.5, BF16)
    for n in range(H_KV):
        sl = slice(n * PAIR, (n + 1) * PAIR)
        qp = q_ref[:, sl] * scale
        zero = jnp.zeros_like(qp)
        lhs = jnp.concatenate([jnp.where(h0, qp, zero), jnp.where(h0, zero, qp)], axis=0)
        s = _dot_nt(lhs, kd_ref[0:t_keys, sl]) + bias2
        m = jnp.max(s, axis=1, keepdims=True)
        p = jnp.exp(s - m)
        l = jnp.sum(p, axis=1, keepdims=True)
        o = _dot(p.astype(BF16), vd_ref[0:t_keys, sl]) / l
        o_ref[:, sl] = jnp.where(h0, o[0:tq], o[tq:]).astype(o_ref.dtype)


def _dsa_prompt_body(n_top, n_buckets, *refs):
    tq = refs[0].shape[0]
    nq = pl.num_programs(1)
    n_tiles = refs[3].shape[0] // tq
    i = pl.program_id(1)
    edges = [round(j * n_tiles / n_buckets) for j in range(n_buckets + 1)]
    for lo, hi in zip(edges[:-1], edges[1:]):
        if hi > lo:
            pl.when((i >= lo) & (i < hi))(functools.partial(_dsa_prompt_tile, n_top, hi * tq, i, *refs))


def _dsa_prompt(proj, n_batch, t_pad, n_top):
    nq = t_pad // Q_TILE
    qrow = lambda w: pl.BlockSpec((Q_TILE, w), lambda b, i: (b * nq + i, 0))
    seq = lambda w: pl.BlockSpec((None, t_pad, w), lambda b, i: (b, 0, 0))
    return pl.pallas_call(
        functools.partial(_dsa_prompt_body, n_top, 4),
        grid=(n_batch, nq),
        in_specs=[qrow(C_A), qrow(H_IDX * D_IDX), qrow(LANES), seq(LANES), seq(2 * C_KV), seq(2 * C_KV)],
        out_specs=qrow(C_A),
        out_shape=jax.ShapeDtypeStruct((n_batch * t_pad, C_A), BF16),
        scratch_shapes=[pltpu.VMEM((Q_TILE, t_pad), I32), pltpu.VMEM((Q_TILE, t_pad), F32)],
        compiler_params=pltpu.CompilerParams(dimension_semantics=("arbitrary", "arbitrary"),
                                             vmem_limit_bytes=VMEM_LIMIT),
        name="dsa_prompt",
    )(proj["q"], proj["qi"], proj["kw"], proj["ki2"], proj["kd"], proj["vd"])


def _ffn_body(tiles_per_seq, stride, last_tile, last_lo, x_ref, yr_ref, ya_ref, g_ref, cp_ref, wbr_ref, wba_ref,
              wo_ref, g2_ref, wup_ref, cw_ref, cb_ref, wdn_ref, gf_ref, y_ref, cl_ref, a_scr):
    i = pl.program_id(0)
    tm, d = x_ref.shape
    d_ff = cb_ref.shape[1]
    hdr = max(SUBLANES, 2 * stride)

    @pl.when(i % tiles_per_seq == 0)
    def _():
        a_scr[hdr - 2 * stride:hdr, :] = cp_ref[...]

    @pl.when(i % tiles_per_seq != 0)
    def _():
        a_scr[hdr - 2 * stride:hdr, :] = a_scr[hdr + tm - 2 * stride:hdr + tm, :]

    g = g_ref[...]
    merged = (jax.nn.sigmoid(g[:, :d]) * _dot(yr_ref[...].astype(BF16), wbr_ref[...])
              + jax.nn.sigmoid(g[:, d:]) * _dot(ya_ref[...].astype(BF16), wba_ref[...]))
    x1 = x_ref[...] + _dot(merged.astype(BF16), wo_ref[...])
    hn = _rms(x1, g2_ref[...]).astype(BF16)
    a_scr[hdr:hdr + tm, :] = _dot(hn, wup_ref[:, :d_ff])
    gate = _dot(hn, wup_ref[:, d_ff:])
    cw = cw_ref[...]
    conv = cb_ref[...] + a_scr[hdr - 2 * stride:hdr - 2 * stride + tm, :] * cw[0:1]
    conv = conv + a_scr[hdr - stride:hdr - stride + tm, :] * cw[1:2]
    conv = conv + a_scr[hdr:hdr + tm, :] * cw[2:3]
    act = 0.5 * conv * (1.0 + lax.erf(conv * (2.0 ** -0.5)))
    x2 = x1 + _dot((act * gate).astype(BF16), wdn_ref[...])
    y_ref[...] = _rms(x2, gf_ref[...])

    @pl.when(i % tiles_per_seq == last_tile)
    def _():
        cl_ref[...] = a_scr[hdr + last_lo:hdr + last_lo + 2 * stride, :]


def _merge_ffn(x, yr, ya, g, conv_prev, lp, gf, tm, tiles_per_seq, stride, t_real):
    n, d = x.shape
    d_ff = lp["conv_b"].shape[-1]
    n_seq = n // (tm * tiles_per_seq)
    first_last = (t_real - 2) * stride
    last_tile, last_lo = first_last // tm, first_last % tm
    hdr = max(SUBLANES, 2 * stride)
    row = lambda w: pl.BlockSpec((tm, w), lambda i: (i, 0))
    weights = [lp["w_br_rwkv"].astype(BF16), lp["w_br_attn"].astype(BF16), lp["w_out"].astype(BF16),
               lp["norm2_g"].reshape(1, d), lp["w_up"].astype(BF16), lp["conv_w"], lp["conv_b"].reshape(1, d_ff),
               lp["w_down"].astype(BF16), gf.reshape(1, d)]
    y, conv_last = pl.pallas_call(
        functools.partial(_ffn_body, tiles_per_seq, stride, last_tile, last_lo),
        grid=(n // tm,),
        in_specs=[row(d), row(C_R), row(C_A), row(2 * d),
                  pl.BlockSpec((None, 2 * stride, d_ff), lambda i: (i // tiles_per_seq, 0, 0))]
                 + [_const_spec(w.shape) for w in weights],
        out_specs=[row(d), pl.BlockSpec((None, 2 * stride, d_ff), lambda i: (i // tiles_per_seq, 0, 0))],
        out_shape=[jax.ShapeDtypeStruct((n, d), F32), jax.ShapeDtypeStruct((n_seq, 2 * stride, d_ff), F32)],
        scratch_shapes=[pltpu.VMEM((hdr + tm, d_ff), F32)],
        compiler_params=pltpu.CompilerParams(dimension_semantics=("arbitrary",), vmem_limit_bytes=VMEM_LIMIT),
        name="merge_convffn",
    )(x, yr, ya, g, conv_prev, *weights)
    return y, conv_last


def _page_copies(pt_ref, batch, first_page, n_pages, srcs, dsts, sems, slot):
    def copies(pg):
        page = pt_ref[batch, first_page + pg]
        rows = srcs[0].shape[-1]
        lanes = pl.ds(pl.multiple_of(pg * rows, rows), rows)
        window = lambda dst: dst.at[(slot,) + (slice(None),) * (len(dst.shape) - 2) + (lanes,)]
        return [pltpu.make_async_copy(src.at[page], window(dst), sems.at[slot, a])
                for a, (src, dst) in enumerate(zip(srcs, dsts))]

    def start():
        def one(pg, carry):
            for cp in copies(pg):
                cp.start()
            return carry
        lax.fori_loop(0, n_pages, one, 0)

    def wait():
        def one(pg, carry):
            for cp in copies(pg):
                cp.wait()
            return carry
        lax.fori_loop(0, n_pages, one, 0)

    return start, wait


def _sample_index_body(n_top, group_pages, pt_ref, qi_ref, w_ref, kin_ref, cache_ref, bias_ref,
                       kbuf, sems, key_scr, bias_scr):
    b = pl.program_id(0)
    nb = pl.num_programs(0)
    page = cache_ref.shape[-1]
    n_keys = kbuf.shape[-1]
    n_pages = n_keys // page
    n_q = key_scr.shape[0]
    slot = b % 2
    fetch = lambda bb, sl: _page_copies(pt_ref, bb, 0, n_pages, [cache_ref], [kbuf], sems, sl)

    @pl.when(b == 0)
    def _():
        fetch(0, 0)[0]()

    @pl.when(b + 1 < nb)
    def _():
        fetch(b + 1, 1 - slot)[0]()

    fetch(b, slot)[1]()

    qi = qi_ref[...]
    w = w_ref[:, 0:1]

    def head_mix(sc):
        sc = jnp.maximum(sc, 0.0) * w
        out = sc[0:n_q]
        for h in range(1, H_IDX):
            out = out + sc[h * n_q:(h + 1) * n_q]
        return out

    gk = group_pages * page
    for g in range(n_pages // group_pages):
        ki_t = kbuf[slot, :, g * gk:(g + 1) * gk].astype(BF16)
        score = head_mix(_dot(qi, ki_t))
        key_scr[:, g * gk:(g + 1) * gk] = _ordered_key(score, jnp.full(score.shape, True))
    score_new = head_mix(_dot_nt(qi, kin_ref[...]))
    vis_new = lax.broadcasted_iota(I32, score_new.shape, 1) <= lax.broadcasted_iota(I32, score_new.shape, 0)
    key_scr[:, n_keys:] = _ordered_key(score_new, vis_new)
    _topk_bias(key_scr, bias_scr, n_top)
    bias_ref[...] = bias_scr[...]


def _sample_attend_body(group_pages, pt_ref, q_ref, bias_ref, kn_ref, vn_ref, ck_ref, cv_ref, o_ref,
                        kbuf, vbuf, sems, m_scr, l_scr, acc_scr):
    b = pl.program_id(0)
    g = pl.program_id(1)
    nb = pl.num_programs(0)
    ng = pl.num_programs(1)
    gk = kbuf.shape[-1]
    step = b * ng + g
    slot = step % 2
    fetch = lambda bb, gg, sl: _page_copies(pt_ref, bb, gg * group_pages, group_pages, [ck_ref, cv_ref],
                                            [kbuf, vbuf], sems, sl)

    @pl.when(step == 0)
    def _():
        fetch(0, 0, 0)[0]()

    @pl.when(step + 1 < nb * ng)
    def _():
        wrap = g + 1 == ng
        fetch(jnp.where(wrap, b + 1, b), jnp.where(wrap, 0, g + 1), 1 - slot)[0]()

    fetch(b, g, slot)[1]()

    @pl.when(g == 0)
    def _():
        m_scr[...] = jnp.full(m_scr.shape, NEG_BIG, F32)
        l_scr[...] = jnp.zeros(l_scr.shape, F32)
        acc_scr[...] = jnp.zeros(acc_scr.shape, F32)

    q = q_ref[...] * jnp.asarray(HD ** -0.5, BF16)
    per_kv = q.shape[1]
    reps = per_kv // bias_ref.shape[0]

    def update(k_t, v_t, bias):
        bias_g = jnp.concatenate([bias] * reps, axis=0)
        s = jnp.concatenate([_dot(q[n], k_t[n]) + bias_g for n in range(H_KV)], axis=0)
        m_old = m_scr[...]
        m_new = jnp.maximum(m_old, jnp.max(s, axis=1, keepdims=True))
        alpha = jnp.exp(m_old - m_new)
        p = jnp.exp(s - m_new)
        l_scr[...] = alpha * l_scr[...] + jnp.sum(p, axis=1, keepdims=True)
        p16 = p.astype(BF16)
        pv = jnp.concatenate([_dot_nt(p16[n * per_kv:(n + 1) * per_kv], v_t[n]) for n in range(H_KV)], axis=0)
        acc_scr[...] = alpha * acc_scr[...] + pv
        m_scr[...] = m_new

    update(kbuf[slot].astype(BF16), vbuf[slot].astype(BF16),
           bias_ref[:, pl.ds(pl.multiple_of(g * gk, LANES), gk)])

    @pl.when(g == ng - 1)
    def _():
        update(kn_ref[...], vn_ref[...], bias_ref[:, ng * gk:])
        o_ref[...] = acc_scr[...] / l_scr[...]


def _dsa_sample(proj, cache_k, cache_v, cache_kidx, page_table, n_batch, n_q, n_top):
    n_pages = page_table.shape[1]
    n_pool, page = cache_kidx.shape[0], cache_kidx.shape[1]
    n_keys = n_pages * page
    width = n_keys + LANES
    rows = H_A * n_q
    per_q = lambda a, w: a.reshape(n_batch, n_q, w)
    heads_first = lambda a: per_q(a, H_A * HD).reshape(n_batch, n_q, H_A, HD).transpose(0, 2, 1, 3)
    pad_keys = lambda a: jnp.pad(a, ((0, 0), (0, LANES - n_q), (0, 0))).astype(BF16)

    qi = heads_first(proj["qi"]).reshape(n_batch, rows, D_IDX)
    wi = per_q(proj["kw"], LANES)[:, :, D_IDX:D_IDX + H_IDX].transpose(0, 2, 1).reshape(n_batch, rows, 1)
    wi = jnp.broadcast_to(wi, (n_batch, rows, LANES))
    ki_new = pad_keys(per_q(proj["kw"], LANES)[:, :, :D_IDX])
    idx_pages = 16
    bias = pl.pallas_call(
        functools.partial(_sample_index_body, n_top, idx_pages),
        grid_spec=pltpu.PrefetchScalarGridSpec(
            num_scalar_prefetch=1, grid=(n_batch,),
            in_specs=[pl.BlockSpec((None, rows, D_IDX), lambda b, pt: (b, 0, 0)),
                      pl.BlockSpec((None, rows, LANES), lambda b, pt: (b, 0, 0)),
                      pl.BlockSpec((None, LANES, D_IDX), lambda b, pt: (b, 0, 0)),
                      pl.BlockSpec(memory_space=pl.ANY)],
            out_specs=pl.BlockSpec((None, n_q, width), lambda b, pt: (b, 0, 0)),
            scratch_shapes=[pltpu.VMEM((2, D_IDX, n_keys), F32), pltpu.SemaphoreType.DMA((2, 1)),
                            pltpu.VMEM((n_q, width), I32), pltpu.VMEM((n_q, width), F32)]),
        out_shape=jax.ShapeDtypeStruct((n_batch, n_q, width), F32),
        compiler_params=pltpu.CompilerParams(dimension_semantics=("arbitrary",), vmem_limit_bytes=VMEM_LIMIT),
        name="dsa_sample_index",
    )(page_table, qi, wi, ki_new, cache_kidx.transpose(0, 2, 1))

    group = H_A // H_KV
    per_kv = group * n_q
    q_kv = per_q(proj["q"], C_A).reshape(n_batch, n_q, H_KV, group, HD).transpose(0, 2, 3, 1, 4)
    q_kv = q_kv.reshape(n_batch, H_KV, per_kv, HD)
    new_t = lambda a: pad_keys(per_q(a, C_KV)).reshape(n_batch, LANES, H_KV, HD).transpose(0, 2, 3, 1)
    att_pages = min(32, n_pages)
    kv_spec = pl.BlockSpec((None, H_KV, HD, LANES), lambda b, g, pt: (b, 0, 0, 0))
    o = pl.pallas_call(
        functools.partial(_sample_attend_body, att_pages),
        grid_spec=pltpu.PrefetchScalarGridSpec(
            num_scalar_prefetch=1, grid=(n_batch, n_pages // att_pages),
            in_specs=[pl.BlockSpec((None, H_KV, per_kv, HD), lambda b, g, pt: (b, 0, 0, 0)),
                      pl.BlockSpec((None, n_q, width), lambda b, g, pt: (b, 0, 0)),
                      kv_spec, kv_spec,
                      pl.BlockSpec(memory_space=pl.ANY), pl.BlockSpec(memory_space=pl.ANY)],
            out_specs=pl.BlockSpec((None, H_KV * per_kv, HD), lambda b, g, pt: (b, 0, 0)),
            scratch_shapes=[pltpu.VMEM((2, H_KV, HD, att_pages * page), F32),
                            pltpu.VMEM((2, H_KV, HD, att_pages * page), F32),
                            pltpu.SemaphoreType.DMA((2, 2)), pltpu.VMEM((rows, 1), F32), pltpu.VMEM((rows, 1), F32),
                            pltpu.VMEM((rows, HD), F32)]),
        out_shape=jax.ShapeDtypeStruct((n_batch, rows, HD), F32),
        compiler_params=pltpu.CompilerParams(dimension_semantics=("arbitrary", "arbitrary"),
                                             vmem_limit_bytes=VMEM_LIMIT),
        name="dsa_sample_attend",
    )(page_table, q_kv, bias, new_t(proj["k"]), new_t(proj["v"]),
      cache_k.transpose(0, 2, 3, 1), cache_v.transpose(0, 2, 3, 1))
    o = o.reshape(n_batch, H_KV, group, n_q, HD).transpose(0, 3, 1, 2, 4)
    return o.reshape(n_batch * n_q, C_A)


def _round_up(x, m):
    return -(-x // m) * m


def kernel(x_prompt, x_sample, cache_k, cache_v, cache_kidx, state_rwkv, state_rwkv_shift, state_ffn_conv, page_table, meta_tokens, norm1_g, w_in, rwkv_mu, rwkv_w0, rwkv_w_w2, rwkv_a0, rwkv_w_a2, rwkv_w_g2, rwkv_k_k, rwkv_k_a, rwkv_r_k, rwkv_lnx_g, rwkv_lnx_b, w_br_rwkv, w_br_attn, w_out, norm2_g, w_up, conv_w, conv_b, w_down, final_norm_g):
    assert w_in.shape[0] == 1, "single-layer model"
    n_b, seq, d = x_prompt.shape
    n_s, n_q, _ = x_sample.shape
    d_ff = conv_b.shape[-1]
    lp = {"rwkv_mu": rwkv_mu[0], "rwkv_w0": rwkv_w0[0], "rwkv_w_w2": rwkv_w_w2[0], "rwkv_a0": rwkv_a0[0],
          "rwkv_w_a2": rwkv_w_a2[0], "rwkv_w_g2": rwkv_w_g2[0], "rwkv_k_k": rwkv_k_k[0], "rwkv_k_a": rwkv_k_a[0],
          "rwkv_r_k": rwkv_r_k[0], "rwkv_lnx_g": rwkv_lnx_g[0], "rwkv_lnx_b": rwkv_lnx_b[0],
          "w_br_rwkv": w_br_rwkv[0], "w_br_attn": w_br_attn[0], "w_out": w_out[0], "norm2_g": norm2_g[0],
          "w_up": w_up[0], "conv_w": conv_w[0], "conv_b": conv_b[0], "w_down": w_down[0]}
    w_packed = _pack_w_in(w_in[0], d)
    g1 = norm1_g[0].reshape(1, d)

    t_real = seq + N_META
    t_pad = _round_up(t_real, Q_TILE)
    tiles_per_seq = 8
    tm = t_pad // tiles_per_seq
    chunk = 64
    meta = jnp.broadcast_to(meta_tokens[None].astype(x_prompt.dtype), (n_b, N_META, d))
    xp = jnp.concatenate([meta, x_prompt, jnp.zeros((n_b, t_pad - t_real, d), x_prompt.dtype)], axis=1)
    xp = xp.reshape(n_b * t_pad, d)
    proj = _project(xp, g1, w_packed, _rope_tables(jnp.arange(t_pad)), tm)
    yr, s_fin = _rwkv(proj["pr"], jnp.zeros((n_b, P_R), F32), jnp.zeros((n_b, H_R // 2, PAIR, PAIR), F32), lp,
                      n_b, t_pad, t_real, chunk, BF16)
    seq3 = lambda a: a.reshape(n_b, t_pad, a.shape[-1])
    ya = _dsa_prompt({k_: (seq3(v_) if k_ in ("ki2", "kd", "vd") else v_) for k_, v_ in proj.items()},
                     n_b, t_pad, min(TOPK_MAX, seq // 4))
    y_p, conv_p = _merge_ffn(xp, yr, ya, proj["g"], jnp.zeros((n_b, CONV_W - 1, d_ff), F32), lp, final_norm_g,
                             tm, tiles_per_seq, 1, t_real)
    out_p = (seq3(y_p)[:, N_META:t_real],
             seq3(proj["k"])[:, :t_real].reshape(1, n_b, t_real, H_KV, HD),
             seq3(proj["v"])[:, :t_real].reshape(1, n_b, t_real, H_KV, HD),
             seq3(proj["kw"])[:, :t_real, :D_IDX][None],
             _state_from_blockdiag(s_fin)[None],
             seq3(proj["pr"])[:, t_real - 1][None],
             conv_p[None])

    past = page_table.shape[1] * cache_kidx.shape[2]
    xs = x_sample.reshape(n_s * n_q, d)
    tab_s = jnp.tile(_rope_tables(past + jnp.arange(n_q)), (n_s, 1))
    proj_s = _project(xs, g1, w_packed, tab_s, n_s * n_q)
    yr_s, s_fin_s = _rwkv(proj_s["pr"], state_rwkv_shift[0], _state_to_blockdiag(state_rwkv[0]), lp,
                          n_s, n_q, n_q, n_q, F32)
    ya_s = _dsa_sample(proj_s, cache_k[0], cache_v[0], cache_kidx[0], page_table, n_s, n_q,
                       min(TOPK_MAX, (past + n_q) // 4))
    time_major = lambda a: a.reshape(n_s, n_q, a.shape[-1]).transpose(1, 0, 2).reshape(n_q * n_s, a.shape[-1])
    conv_prev_s = state_ffn_conv[0].transpose(1, 0, 2).reshape(1, (CONV_W - 1) * n_s, d_ff)
    y_s, conv_s = _merge_ffn(time_major(xs), time_major(yr_s), time_major(ya_s), time_major(proj_s["g"]),
                             conv_prev_s, lp, final_norm_g, n_s * n_q, 1, n_s, n_q)
    per_q = lambda a: a.reshape(n_s, n_q, a.shape[-1])
    out_s = (y_s.reshape(n_q, n_s, d).transpose(1, 0, 2),
             per_q(proj_s["k"]).reshape(1, n_s, n_q, H_KV, HD),
             per_q(proj_s["v"]).reshape(1, n_s, n_q, H_KV, HD),
             per_q(proj_s["kw"])[:, :, :D_IDX][None],
             _state_from_blockdiag(s_fin_s)[None],
             per_q(proj_s["pr"])[:, n_q - 1][None],
             conv_s.reshape(CONV_W - 1, n_s, d_ff).transpose(1, 0, 2)[None])
    return (out_p[0], out_s[0]) + out_p[1:] + out_s[1:]
```

```python
import functools
import math

import jax
import jax.numpy as jnp
import numpy as np
from jax import lax
from jax.experimental import pallas as pl
from jax.experimental.pallas import tpu as pltpu

F32 = jnp.float32
BF16 = jnp.bfloat16
I32 = jnp.int32

LANES = 128
SUBLANES = 8
VMEM_LIMIT = 56 * 1024 * 1024

N_META = 16
HD = 64
PAIR = 2 * HD
H_R = 8
C_R = H_R * HD
D_W_LORA, D_A_LORA, D_G_LORA = 64, 64, 128
P_R = 3 * C_R + D_W_LORA + D_A_LORA + D_G_LORA
LNX_EPS = 64e-5
H_A, H_KV = 8, 4
C_A, C_KV = H_A * HD, H_KV * HD
H_IDX, D_IDX = 8, 64
TOPK_MAX = 256
ROPE_THETA = 500000.0
ROT = HD // 4
ROT_HALF = ROT // 2
RMS_EPS = 1e-6
CONV_W = 3
Q_TILE = 128
INT_MIN = -(2 ** 31)
NEG_BIG = -1e30

_GROUPS = (("pr", P_R), ("q", C_A), ("qi", H_IDX * D_IDX), ("kd", 2 * C_KV), ("vd", 2 * C_KV),
           ("k", C_KV), ("v", C_KV), ("kw", LANES), ("ki2", LANES), ("g", None))


def _group_offsets(d_model):
    offs, o = {}, 0
    for name, width in _GROUPS:
        width = 2 * d_model if width is None else width
        offs[name] = (o, o + width)
        o += width
    return offs, o


def _pack_w_in(w_in, d_model):
    o = P_R
    q = w_in[:, o:o + C_A]; o += C_A
    k = w_in[:, o:o + C_KV]; o += C_KV
    v = w_in[:, o:o + C_KV]; o += C_KV
    qi = w_in[:, o:o + H_IDX * D_IDX]; o += H_IDX * D_IDX
    wi = w_in[:, o:o + H_IDX]; o += H_IDX
    ki = w_in[:, o:o + D_IDX]; o += D_IDX
    g = w_in[:, o:o + 2 * d_model]
    dup = lambda t: jnp.concatenate([t[:, (n // 2) * HD:(n // 2 + 1) * HD] for n in range(2 * H_KV)], axis=1)
    kw = jnp.concatenate([ki, wi, jnp.zeros((w_in.shape[0], LANES - D_IDX - H_IDX), w_in.dtype)], axis=1)
    packed = jnp.concatenate([w_in[:, :P_R], q, qi, dup(k), dup(v), k, v, kw, jnp.concatenate([ki, ki], axis=1), g],
                             axis=1)
    return packed.astype(BF16)


def _rope_tables(pos):
    inv = ROPE_THETA ** (-jnp.arange(ROT_HALF, dtype=F32) / ROT_HALF)
    ang = pos.astype(F32)[:, None] * inv[None, :]
    cos, sin = jnp.cos(ang), jnp.sin(ang)
    n = pos.shape[0]
    one = jnp.ones((n, HD - ROT), F32)
    zero = jnp.zeros((n, HD - ROT_HALF), F32)
    c64 = jnp.concatenate([cos, cos, one], axis=1)
    s1_64 = jnp.concatenate([-sin, zero], axis=1)
    s2_64 = jnp.concatenate([jnp.zeros((n, ROT_HALF), F32), sin, jnp.zeros((n, HD - ROT), F32)], axis=1)
    wi_scale = jnp.full((n, H_IDX), (H_IDX ** -0.5) * (D_IDX ** -0.5), F32)
    hi_c = jnp.concatenate([wi_scale, jnp.ones((n, HD - H_IDX), F32)], axis=1)
    z64 = jnp.zeros((n, HD), F32)
    return jnp.concatenate([c64, c64, s1_64, s1_64, s2_64, s2_64,
                            c64, hi_c, s1_64, z64, s2_64, z64], axis=1)


def _const_spec(shape):
    nd = len(shape)
    return pl.BlockSpec(shape, lambda *_: (0,) * nd, pipeline_mode=pl.Buffered(1))


def _half_masks(rows):
    lane = lax.broadcasted_iota(I32, (rows, PAIR), 1)
    return lane < HD


def _rms(x, g):
    return x * lax.rsqrt(jnp.mean(x * x, axis=-1, keepdims=True) + RMS_EPS) * g


def _dot(a, b):
    return jnp.dot(a, b, preferred_element_type=F32)


def _dot_nt(a, b):
    return lax.dot_general(a, b, (((1,), (1,)), ((), ())), preferred_element_type=F32)


def _dot_tn(a, b):
    return lax.dot_general(a, b, (((0,), (0,)), ((), ())), preferred_element_type=F32)


def _rope(h, c, s1, s2):
    outs = []
    for j in range(h.shape[1] // LANES):
        hj = h[:, j * LANES:(j + 1) * LANES]
        outs.append(hj * c + pltpu.roll(hj, LANES - ROT_HALF, 1) * s1 + pltpu.roll(hj, ROT_HALF, 1) * s2)
    return outs[0] if len(outs) == 1 else jnp.concatenate(outs, axis=1)


def _proj_body(offs, x_ref, g1_ref, w_ref, tab_ref, pr_o, q_o, qi_o, kd_o, vd_o, k_o, v_o, kw_o, ki2_o, g_o):
    xn = _rms(x_ref[...], g1_ref[...]).astype(BF16)
    mm = lambda name: _dot(xn, w_ref[:, offs[name][0]:offs[name][1]])
    tab = tab_ref[...]
    c, s1, s2 = (tab[:, i * LANES:(i + 1) * LANES] for i in range(3))
    ck, s1k, s2k = (tab[:, i * LANES:(i + 1) * LANES] for i in range(3, 6))
    pr_o[...] = mm("pr")
    g_o[...] = mm("g")
    v_o[...] = mm("v")
    vd_o[...] = mm("vd").astype(BF16)
    q_o[...] = _rope(mm("q"), c, s1, s2).astype(BF16)
    qi_o[...] = _rope(mm("qi"), c, s1, s2).astype(BF16)
    kd_o[...] = _rope(mm("kd"), c, s1, s2).astype(BF16)
    k_o[...] = _rope(mm("k"), c, s1, s2)
    kw_o[...] = _rope(mm("kw"), ck, s1k, s2k)
    ki2_o[...] = _rope(mm("ki2"), c, s1, s2).astype(BF16)


def _project(x, g1, w_packed, tab, tm):
    n, d = x.shape
    offs, n_cols = _group_offsets(d)
    period_tiles = tab.shape[0] // tm
    row = lambda w: pl.BlockSpec((tm, w), lambda i: (i, 0))
    widths = [(name, hi - lo) for name, (lo, hi) in offs.items()]
    dtypes = {"pr": F32, "q": BF16, "qi": BF16, "kd": BF16, "vd": BF16, "k": F32, "v": F32, "kw": F32,
              "ki2": BF16, "g": F32}
    outs = pl.pallas_call(
        functools.partial(_proj_body, offs),
        grid=(n // tm,),
        in_specs=[row(d), _const_spec((1, d)), _const_spec((d, n_cols)),
                  pl.BlockSpec((tm, tab.shape[1]), lambda i: (i % period_tiles, 0))],
        out_specs=[row(w) for _, w in widths],
        out_shape=[jax.ShapeDtypeStruct((n, w), dtypes[name]) for name, w in widths],
        compiler_params=pltpu.CompilerParams(dimension_semantics=("arbitrary",), vmem_limit_bytes=VMEM_LIMIT),
        name="in_proj",
    )(x, g1, w_packed, tab)
    return dict(zip([name for name, _ in widths], outs))


def _head_sum(x, h0):
    outs = []
    for p in range(x.shape[1] // PAIR):
        xp = x[:, p * PAIR:(p + 1) * PAIR]
        s0 = jnp.sum(jnp.where(h0, xp, 0.0), axis=1, keepdims=True)
        s1 = jnp.sum(jnp.where(h0, 0.0, xp), axis=1, keepdims=True)
        outs.append(jnp.where(h0, s0, s1))
    return jnp.concatenate(outs, axis=1)


def _rwkv_prep(chunk, valid, x, prev, mu_ref, w0_ref, ww2_ref, a0_ref, wa2_ref, wg2_ref, kk_ref, ka_ref):
    pm = x + (prev - x) * mu_ref[...]

    r = pm[:, 0:C_R]
    k = pm[:, C_R:2 * C_R]
    v = pm[:, 2 * C_R:3 * C_R]
    o = 3 * C_R
    wd = pm[:, o:o + D_W_LORA]; o += D_W_LORA
    ad = pm[:, o:o + D_A_LORA]; o += D_A_LORA
    gd = pm[:, o:o + D_G_LORA]

    z = -(w0_ref[...] + _dot(jnp.tanh(wd).astype(BF16), ww2_ref[...]))
    softplus = jnp.maximum(z, 0.0) + jnp.log1p(jnp.exp(-jnp.abs(z)))
    logdec = -jnp.exp(-softplus - 0.5)
    gate = jax.nn.sigmoid(a0_ref[...] + _dot(ad.astype(BF16), wa2_ref[...]))
    g_out = _dot(jax.nn.sigmoid(gd).astype(BF16), wg2_ref[...])

    h0 = _half_masks(chunk)
    kk = k * kk_ref[...]
    kk = kk * lax.rsqrt(jnp.maximum(_head_sum(kk * kk, h0), 1e-24))
    k2 = k * (1.0 + (gate - 1.0) * ka_ref[...])

    kk = jnp.where(valid, kk, 0.0)
    k2m = jnp.where(valid, k2, 0.0)
    logdec = jnp.where(valid, logdec, 0.0)

    ri = lax.broadcasted_iota(I32, (chunk, chunk), 0)
    ci = lax.broadcasted_iota(I32, (chunk, chunk), 1)
    tri16 = jnp.where(ri >= ci, 1.0, 0.0).astype(BF16)
    ld_hi = logdec.astype(BF16)
    ld_r = logdec - ld_hi.astype(F32)
    ld_mid = ld_r.astype(BF16)
    ld_lo = (ld_r - ld_mid.astype(F32)).astype(BF16)
    cum = _dot(tri16, ld_hi) + _dot(tri16, ld_mid) + _dot(tri16, ld_lo)
    cum_last = cum[chunk - 1:chunk, :]
    b = kk * gate
    a_t = -kk * jnp.exp(cum - logdec)
    e_neg = jnp.exp(-cum)
    b_t = b * e_neg
    k_t = k2m * e_neg
    r_t = r * jnp.exp(cum)
    e_tail = jnp.exp(cum_last - cum)
    b_g = b * e_tail
    k_g = k2m * e_tail
    g_last = jnp.exp(cum_last)
    return dict(r=r, k2=k2, v=v, g_out=g_out, a_t=a_t, b_t=b_t, k_t=k_t, r_t=r_t, b_g=b_g, k_g=k_g, g_last=g_last)


def _rwkv_body(t_real, chunk, pr_ref, sh_ref, st_ref, mu_ref, w0_ref, ww2_ref, a0_ref, wa2_ref, wg2_ref,
               kk_ref, ka_ref, rk_ref, lg_ref, lb_ref, y_ref, so_ref, s_scr, xs_scr):
    c = pl.program_id(1)
    n_chunks = pl.num_programs(1)
    n_seq = pr_ref.shape[0]
    hdr = SUBLANES

    @pl.when(c == 0)
    def _():
        s_scr[...] = st_ref[...]
        xs_scr[:, hdr - 1:hdr, :] = sh_ref[...]

    @pl.when(c > 0)
    def _():
        xs_scr[:, hdr - 1:hdr, :] = xs_scr[:, hdr + chunk - 1:hdr + chunk, :]

    valid = c * chunk + lax.broadcasted_iota(I32, (chunk, 1), 0) < t_real
    seqs = []
    for i in range(n_seq):
        x = pr_ref[i]
        xs_scr[i, hdr:hdr + chunk, :] = x
        prev = xs_scr[i, hdr - 1:hdr - 1 + chunk, :]
        seqs.append(_rwkv_prep(chunk, valid, x, prev, mu_ref, w0_ref, ww2_ref, a0_ref, wa2_ref, wg2_ref,
                               kk_ref, ka_ref))

    c2 = 2 * chunk
    units = [(i, p) for i in range(n_seq) for p in range(H_R // 2)]
    idx = range(len(units))
    h0 = _half_masks(chunk)
    split = lambda x: jnp.concatenate([jnp.where(h0, x, 0.0), jnp.where(h0, 0.0, x)], axis=0)
    split16 = lambda x: split(x).astype(BF16)
    cols = lambda x, p: x[:, p * PAIR:(p + 1) * PAIR]
    get = lambda name, u: cols(seqs[units[u][0]][name], units[u][1])
    rb = lax.broadcasted_iota(I32, (c2, c2), 0)
    cb = lax.broadcasted_iota(I32, (c2, c2), 1)
    same = (rb < chunk) == (cb < chunk)
    strict = same & (rb > cb)
    incl = same & (rb >= cb)
    eye = (rb == cb).astype(F32)
    n_sq = max(int(math.log2(chunk)) - 1, 0)

    a_s = [split16(get("a_t", u)) for u in idx]
    r_f = [split(get("r_t", u)) for u in idx]
    r_s = [r_f[u].astype(BF16) for u in idx]
    v_s = [split16(get("v", u)) for u in idx]
    bg_s = [split16(get("b_g", u)) for u in idx]
    kg_s = [split16(get("k_g", u)) for u in idx]
    xx = [_dot_nt(jnp.concatenate([a_s[u], r_s[u]], axis=0),
                  jnp.concatenate([split16(get("b_t", u)), split16(get("k_t", u))], axis=0)) for u in idx]
    l_ab = [jnp.where(strict, xx[u][0:c2, 0:c2], 0.0) for u in idx]
    l_ak = [jnp.where(strict, xx[u][0:c2, c2:], 0.0).astype(BF16) for u in idx]
    t_rb = [jnp.where(incl, xx[u][c2:, 0:c2], 0.0).astype(BF16) for u in idx]
    t_rk = [jnp.where(incl, xx[u][c2:, c2:], 0.0).astype(BF16) for u in idx]
    inv = [eye + l_ab[u] for u in idx]
    lp = l_ab
    for _ in range(n_sq):
        lp16 = [lp[u].astype(BF16) for u in idx]
        lp = [_dot(lp16[u], lp16[u]) for u in idx]
        inv = [inv[u] + _dot(inv[u].astype(BF16), lp[u].astype(BF16)) for u in idx]
    inv = [inv[u].astype(BF16) for u in idx]
    a_hat = [_dot(inv[u], a_s[u]).astype(BF16) for u in idx]
    w_s = [_dot(l_ak[u], v_s[u]).astype(BF16) for u in idx]
    u0 = [_dot(inv[u], w_s[u]).astype(BF16) for u in idx]
    r_hat = [(r_f[u] + _dot(t_rb[u], a_hat[u])).astype(BF16) for u in idx]
    y0 = [_dot(t_rb[u], u0[u]) + _dot(t_rk[u], v_s[u]) for u in idx]
    trans = [_dot_tn(a_hat[u], bg_s[u]).astype(BF16) for u in idx]
    add = [_dot_tn(u0[u], bg_s[u]) + _dot_tn(v_s[u], kg_s[u]) for u in idx]
    ys = [[] for _ in range(n_seq)]
    for u, (i, p) in enumerate(units):
        s_old = s_scr[i, p]
        s16 = s_old.astype(BF16)
        y_split = _dot_nt(r_hat[u], s16) + y0[u]
        ys[i].append(y_split[0:chunk] + y_split[chunk:])
        s_scr[i, p] = s_old * get("g_last", u) + _dot(s16, trans[u]) + add[u]

    for i in range(n_seq):
        y = jnp.concatenate(ys[i], axis=1)
        sq = seqs[i]
        mean = _head_sum(y, h0) * (1.0 / HD)
        d = y - mean
        var = _head_sum(d * d, h0) * (1.0 / HD)
        yn = d * lax.rsqrt(var + LNX_EPS) * lg_ref[...] + lb_ref[...]
        yn = yn + _head_sum(sq["r"] * sq["k2"] * rk_ref[...], h0) * sq["v"]
        y_ref[i] = (yn * sq["g_out"]).astype(y_ref.dtype)

    @pl.when(c == n_chunks - 1)
    def _():
        so_ref[...] = s_scr[...]


def _rwkv(pr, shift_prev, state_bd, lp, n_batch, t_pad, t_real, chunk, y_dtype):
    n_chunks = t_pad // chunk
    n_pairs = H_R // 2
    vec = lambda name, w: lp[name].reshape(1, w).astype(F32)
    params = [vec("rwkv_mu", P_R), vec("rwkv_w0", C_R), lp["rwkv_w_w2"].astype(BF16), vec("rwkv_a0", C_R),
              lp["rwkv_w_a2"].astype(BF16), lp["rwkv_w_g2"].astype(BF16), vec("rwkv_k_k", C_R),
              vec("rwkv_k_a", C_R), vec("rwkv_r_k", C_R), vec("rwkv_lnx_g", C_R), vec("rwkv_lnx_b", C_R)]
    n_seq = next(n for n in (4, 2, 1) if n_batch % n == 0)
    y, s_out = pl.pallas_call(
        functools.partial(_rwkv_body, t_real, chunk),
        grid=(n_batch // n_seq, n_chunks),
        in_specs=[pl.BlockSpec((n_seq, chunk, P_R), lambda b, c: (b, c, 0)),
                  pl.BlockSpec((n_seq, 1, P_R), lambda b, c: (b, 0, 0)),
                  pl.BlockSpec((n_seq, n_pairs, PAIR, PAIR), lambda b, c: (b, 0, 0, 0))]
                 + [_const_spec(p.shape) for p in params],
        out_specs=[pl.BlockSpec((n_seq, chunk, C_R), lambda b, c: (b, c, 0)),
                   pl.BlockSpec((n_seq, n_pairs, PAIR, PAIR), lambda b, c: (b, 0, 0, 0))],
        out_shape=[jax.ShapeDtypeStruct((n_batch, t_pad, C_R), y_dtype),
                   jax.ShapeDtypeStruct((n_batch, n_pairs, PAIR, PAIR), F32)],
        scratch_shapes=[pltpu.VMEM((n_seq, n_pairs, PAIR, PAIR), F32),
                        pltpu.VMEM((n_seq, SUBLANES + chunk, P_R), F32)],
        compiler_params=pltpu.CompilerParams(dimension_semantics=("arbitrary", "arbitrary"),
                                             vmem_limit_bytes=VMEM_LIMIT),
        name="rwkv7_chunked",
    )(pr.reshape(n_batch, t_pad, P_R), shift_prev.reshape(n_batch, 1, P_R), state_bd, *params)
    return y.reshape(n_batch * t_pad, C_R), s_out


def _state_to_blockdiag(s):
    b = s.shape[0]
    s = s.reshape(b, H_R // 2, 2, HD, HD)
    z = jnp.zeros_like(s[:, :, 0])
    top = jnp.concatenate([s[:, :, 0], z], axis=-1)
    bot = jnp.concatenate([z, s[:, :, 1]], axis=-1)
    return jnp.concatenate([top, bot], axis=-2)


def _state_from_blockdiag(s):
    b = s.shape[0]
    return jnp.stack([s[:, :, :HD, :HD], s[:, :, HD:, HD:]], axis=2).reshape(b, H_R, HD, HD)


def _ordered_key(score, visible):
    bits = pltpu.bitcast(score + 0.0, I32)
    key = jnp.where(bits < 0, bits ^ jnp.int32(0x7FFFFFFF), bits)
    return jnp.where(visible, key, jnp.int32(INT_MIN))


def _kth_largest(count_ge, n_bits, k, rows, width):
    def step(it, carry):
        prefix, n_ge = carry
        trial = prefix | jnp.left_shift(jnp.int32(1), jnp.int32(n_bits - 1) - it)
        cnt = count_ge(trial)
        take = cnt >= k
        return jnp.where(take, trial, prefix), jnp.where(take, cnt, n_ge)

    return lax.fori_loop(0, n_bits, step, (jnp.zeros((rows, 1), I32), jnp.full((rows, 1), width, F32)), unroll=8)


def _count16(ref16, hit):
    acc = None
    for j in range(ref16.shape[1] // LANES):
        one = jnp.where(hit(ref16[:, j * LANES:(j + 1) * LANES]), jnp.int16(1), jnp.int16(0))
        acc = one if acc is None else acc + one
    return jnp.sum(acc.astype(F32), axis=1, keepdims=True)


def _topk_bias(key_scr, bias_scr, n_top, half_scr=None):
    rows, width = key_scr.shape
    k_f = jnp.float32(n_top)
    n_parts = max(1, min(width // LANES, (8 * SUBLANES) // rows))
    bounds = [round(j * (width // LANES) / n_parts) * LANES for j in range(n_parts + 1)]

    def count_ge(thr):
        parts = [jnp.sum(jnp.where(key_scr[:, lo:hi] >= thr, 1.0, 0.0), axis=1, keepdims=True)
                 for lo, hi in zip(bounds[:-1], bounds[1:])]
        return functools.reduce(lambda a, b: a + b, parts)

    if half_scr is None:
        prefix, n_ge = _kth_largest(lambda t: count_ge(t ^ jnp.int32(INT_MIN)), 32, k_f, rows, width)
        thr = prefix ^ jnp.int32(INT_MIN)
    else:
        hi_scr, lo_scr = half_scr
        bias16 = 1 << 15
        signed16 = lambda biased: (biased - bias16).astype(jnp.int16)
        key = key_scr[...]
        hi_scr[...] = lax.shift_right_arithmetic(key, 16).astype(jnp.int16)
        top, _ = _kth_largest(lambda t: _count16(hi_scr, lambda blk: blk >= signed16(t)), 16, k_f, rows, width)
        top16 = signed16(top)
        n_above = _count16(hi_scr, lambda blk: blk > top16)
        low = ((key & jnp.int32(0xFFFF)) - bias16).astype(jnp.int16)
        lo_scr[...] = jnp.where(hi_scr[...] == top16, low, jnp.int16(-bias16))
        bottom, _ = _kth_largest(lambda t: _count16(lo_scr, lambda blk: blk >= signed16(t)), 16, k_f - n_above,
                                 rows, width)
        thr = jnp.left_shift(top - bias16, 16) | bottom
        n_ge = count_ge(thr)
    bias_scr[...] = jnp.where(key_scr[...] >= jnp.maximum(thr, jnp.int32(INT_MIN + 1)), 0.0, -jnp.inf)
    ambiguous = (n_ge > k_f) & (thr > jnp.int32(INT_MIN))

    @pl.when(jnp.max(jnp.where(ambiguous, 1.0, 0.0)) > 0.0)
    def _():
        upper = (lax.broadcasted_iota(I32, (LANES, LANES), 0)
                 < lax.broadcasted_iota(I32, (LANES, LANES), 1)).astype(BF16)
        n_gt = jnp.sum(jnp.where(key_scr[...] > thr, 1.0, 0.0), axis=1, keepdims=True)
        need = k_f - n_gt

        def block(kb, seen):
            lo = pl.multiple_of(kb * LANES, LANES)
            kblk = key_scr[:, pl.ds(lo, LANES)]
            eq = kblk == thr
            eq16 = jnp.where(eq, 1.0, 0.0).astype(BF16)
            rank = seen + _dot(eq16, upper)
            take = (kblk > thr) | (eq & (rank < need))
            bias_scr[:, pl.ds(lo, LANES)] = jnp.where(take & (kblk > jnp.int32(INT_MIN)), 0.0, -jnp.inf)
            return seen + jnp.sum(jnp.where(eq, 1.0, 0.0), axis=1, keepdims=True)

        lax.fori_loop(0, width // LANES, block, jnp.zeros((rows, 1), F32))


def _indexer_scores(qi_ref, ki2, wi, h0):
    rows = qi_ref.shape[0]
    score = None
    for p in range(H_IDX // 2):
        qp = qi_ref[:, p * PAIR:(p + 1) * PAIR]
        zero = jnp.zeros_like(qp)
        lhs = jnp.concatenate([jnp.where(h0, qp, zero), jnp.where(h0, zero, qp)], axis=0)
        sc = jnp.maximum(_dot_nt(lhs, ki2), 0.0)
        part = wi[:, 2 * p:2 * p + 1] * sc[0:rows] + wi[:, 2 * p + 1:2 * p + 2] * sc[rows:]
        score = part if score is None else score + part
    return score


def _dsa_prompt_tile(n_top, t_keys, i, q_ref, qi_ref, kw_ref, ki2_ref, kd_ref, vd_ref, o_ref, key_scr, bias_scr,
                     hi_scr, lo_scr):
    n_seq, tq = q_ref.shape[0], q_ref.shape[1]
    h0 = _half_masks(tq)
    q_pos = i * tq + lax.broadcasted_iota(I32, (tq, 1), 0)
    visible = lax.broadcasted_iota(I32, (tq, t_keys), 1) <= q_pos
    key_scr = key_scr.at[:, 0:t_keys]
    bias_scr = bias_scr.at[:, 0:t_keys]

    for j in range(n_seq):
        wi = kw_ref[j, :, D_IDX:D_IDX + H_IDX]
        score = _indexer_scores(qi_ref.at[j], ki2_ref[j, 0:t_keys, :], wi, h0)
        key_scr[j * tq:(j + 1) * tq, :] = _ordered_key(score, visible)
    _topk_bias(key_scr, bias_scr, n_top, (hi_scr.at[:, 0:t_keys], lo_scr.at[:, 0:t_keys]))

    scale = jnp.asarray(HD ** -0.5, BF16)
    for j in range(n_seq):
        bias = bias_scr[j * tq:(j + 1) * tq, :]
        bias2 = jnp.concatenate([bias, bias], axis=0)
        for n in range(H_KV):
            sl = slice(n * PAIR, (n + 1) * PAIR)
            qp = q_ref[j, :, sl] * scale
            zero = jnp.zeros_like(qp)
            lhs = jnp.concatenate([jnp.where(h0, qp, zero), jnp.where(h0, zero, qp)], axis=0)
            s = _dot_nt(lhs, kd_ref[j, 0:t_keys, sl]) + bias2
            m = jnp.max(s, axis=1, keepdims=True)
            p = jnp.exp(s - m)
            l = jnp.sum(p, axis=1, keepdims=True)
            o = _dot(p.astype(BF16), vd_ref[j, 0:t_keys, sl]) / l
            o_ref[j, :, sl] = jnp.where(h0, o[0:tq], o[tq:]).astype(o_ref.dtype)


def _dsa_prompt_body(n_top, n_buckets, *refs):
    tq = refs[0].shape[1]
    n_tiles = refs[3].shape[1] // tq
    i = pl.program_id(1)
    edges = [round(j * n_tiles / n_buckets) for j in range(n_buckets + 1)]
    for lo, hi in zip(edges[:-1], edges[1:]):
        if hi > lo:
            pl.when((i >= lo) & (i < hi))(functools.partial(_dsa_prompt_tile, n_top, hi * tq, i, *refs))


def _dsa_prompt(proj, n_batch, t_pad, n_top):
    nq = t_pad // Q_TILE
    n_seq = 2 if n_batch % 2 == 0 else 1
    rows = n_seq * Q_TILE
    qrow = lambda w: pl.BlockSpec((n_seq, Q_TILE, w), lambda b, i: (b, i, 0))
    seq = lambda w: pl.BlockSpec((n_seq, t_pad, w), lambda b, i: (b, 0, 0))
    per_seq = lambda name: proj[name].reshape(n_batch, t_pad, proj[name].shape[-1])
    ya = pl.pallas_call(
        functools.partial(_dsa_prompt_body, n_top, 4),
        grid=(n_batch // n_seq, nq),
        in_specs=[qrow(C_A), qrow(H_IDX * D_IDX), qrow(LANES), seq(LANES), seq(2 * C_KV), seq(2 * C_KV)],
        out_specs=qrow(C_A),
        out_shape=jax.ShapeDtypeStruct((n_batch, t_pad, C_A), BF16),
        scratch_shapes=[pltpu.VMEM((rows, t_pad), I32), pltpu.VMEM((rows, t_pad), F32),
                        pltpu.VMEM((rows, t_pad), jnp.int16), pltpu.VMEM((rows, t_pad), jnp.int16)],
        compiler_params=pltpu.CompilerParams(dimension_semantics=("arbitrary", "arbitrary"),
                                             vmem_limit_bytes=VMEM_LIMIT),
        name="dsa_prompt",
    )(*[per_seq(name) for name in ("q", "qi", "kw", "ki2", "kd", "vd")])
    return ya.reshape(n_batch * t_pad, C_A)


def _ffn_body(tiles_per_seq, stride, last_tile, last_lo, x_ref, yr_ref, ya_ref, g_ref, cp_ref, wbr_ref, wba_ref,
              wo_ref, g2_ref, wup_ref, cw_ref, cb_ref, wdn_ref, gf_ref, y_ref, cl_ref, a_scr):
    i = pl.program_id(0)
    tm, d = x_ref.shape
    d_ff = cb_ref.shape[1]
    hdr = max(SUBLANES, 2 * stride)

    @pl.when(i % tiles_per_seq == 0)
    def _():
        a_scr[hdr - 2 * stride:hdr, :] = cp_ref[...]

    @pl.when(i % tiles_per_seq != 0)
    def _():
        a_scr[hdr - 2 * stride:hdr, :] = a_scr[hdr + tm - 2 * stride:hdr + tm, :]

    g = g_ref[...]
    merged = (jax.nn.sigmoid(g[:, :d]) * _dot(yr_ref[...].astype(BF16), wbr_ref[...])
              + jax.nn.sigmoid(g[:, d:]) * _dot(ya_ref[...].astype(BF16), wba_ref[...]))
    x1 = x_ref[...] + _dot(merged.astype(BF16), wo_ref[...])
    hn = _rms(x1, g2_ref[...]).astype(BF16)
    a_scr[hdr:hdr + tm, :] = _dot(hn, wup_ref[:, :d_ff])
    gate = _dot(hn, wup_ref[:, d_ff:])
    cw = cw_ref[...]
    conv = cb_ref[...] + a_scr[hdr - 2 * stride:hdr - 2 * stride + tm, :] * cw[0:1]
    conv = conv + a_scr[hdr - stride:hdr - stride + tm, :] * cw[1:2]
    conv = conv + a_scr[hdr:hdr + tm, :] * cw[2:3]
    act = 0.5 * conv * (1.0 + lax.erf(conv * (2.0 ** -0.5)))
    x2 = x1 + _dot((act * gate).astype(BF16), wdn_ref[...])
    y_ref[...] = _rms(x2, gf_ref[...])

    @pl.when(i % tiles_per_seq == last_tile)
    def _():
        cl_ref[...] = a_scr[hdr + last_lo:hdr + last_lo + 2 * stride, :]


def _merge_ffn(x, yr, ya, g, conv_prev, lp, gf, tm, tiles_per_seq, stride, t_real):
    n, d = x.shape
    d_ff = lp["conv_b"].shape[-1]
    n_seq = n // (tm * tiles_per_seq)
    first_last = (t_real - 2) * stride
    last_tile, last_lo = first_last // tm, first_last % tm
    hdr = max(SUBLANES, 2 * stride)
    row = lambda w: pl.BlockSpec((tm, w), lambda i: (i, 0))
    weights = [lp["w_br_rwkv"].astype(BF16), lp["w_br_attn"].astype(BF16), lp["w_out"].astype(BF16),
               lp["norm2_g"].reshape(1, d), lp["w_up"].astype(BF16), lp["conv_w"], lp["conv_b"].reshape(1, d_ff),
               lp["w_down"].astype(BF16), gf.reshape(1, d)]
    y, conv_last = pl.pallas_call(
        functools.partial(_ffn_body, tiles_per_seq, stride, last_tile, last_lo),
        grid=(n // tm,),
        in_specs=[row(d), row(C_R), row(C_A), row(2 * d),
                  pl.BlockSpec((None, 2 * stride, d_ff), lambda i: (i // tiles_per_seq, 0, 0))]
                 + [_const_spec(w.shape) for w in weights],
        out_specs=[row(d), pl.BlockSpec((None, 2 * stride, d_ff), lambda i: (i // tiles_per_seq, 0, 0))],
        out_shape=[jax.ShapeDtypeStruct((n, d), F32), jax.ShapeDtypeStruct((n_seq, 2 * stride, d_ff), F32)],
        scratch_shapes=[pltpu.VMEM((hdr + tm, d_ff), F32)],
        compiler_params=pltpu.CompilerParams(dimension_semantics=("arbitrary",), vmem_limit_bytes=VMEM_LIMIT),
        name="merge_convffn",
    )(x, yr, ya, g, conv_prev, *weights)
    return y, conv_last


def _page_copies(pt_ref, batch, first_page, n_pages, srcs, dsts, sems, slot):
    def copies(pg):
        page = pt_ref[batch, first_page + pg]
        rows = srcs[0].shape[-1]
        lanes = pl.ds(pl.multiple_of(pg * rows, rows), rows)
        window = lambda dst: dst.at[(slot,) + (slice(None),) * (len(dst.shape) - 2) + (lanes,)]
        return [pltpu.make_async_copy(src.at[page], window(dst), sems.at[slot, a])
                for a, (src, dst) in enumerate(zip(srcs, dsts))]

    def start():
        def one(pg, carry):
            for cp in copies(pg):
                cp.start()
            return carry
        lax.fori_loop(0, n_pages, one, 0)

    def wait():
        def one(pg, carry):
            for cp in copies(pg):
                cp.wait()
            return carry
        lax.fori_loop(0, n_pages, one, 0)

    return start, wait


def _sample_index_body(n_top, group_pages, pt_ref, qi_ref, w_ref, kin_ref, cache_ref, bias_ref,
                       kbuf, sems, key_scr, bias_scr):
    b = pl.program_id(0)
    nb = pl.num_programs(0)
    page = cache_ref.shape[-1]
    n_keys = kbuf.shape[-1]
    n_pages = n_keys // page
    n_q = key_scr.shape[0]
    slot = b % 2
    fetch = lambda bb, sl: _page_copies(pt_ref, bb, 0, n_pages, [cache_ref], [kbuf], sems, sl)

    @pl.when(b == 0)
    def _():
        fetch(0, 0)[0]()

    @pl.when(b + 1 < nb)
    def _():
        fetch(b + 1, 1 - slot)[0]()

    fetch(b, slot)[1]()

    qi = qi_ref[...]
    w = w_ref[:, 0:1]

    def head_mix(sc):
        sc = jnp.maximum(sc, 0.0) * w
        out = sc[0:n_q]
        for h in range(1, H_IDX):
            out = out + sc[h * n_q:(h + 1) * n_q]
        return out

    gk = group_pages * page
    for g in range(n_pages // group_pages):
        ki_t = kbuf[slot, :, g * gk:(g + 1) * gk].astype(BF16)
        score = head_mix(_dot(qi, ki_t))
        key_scr[:, g * gk:(g + 1) * gk] = _ordered_key(score, jnp.full(score.shape, True))
    score_new = head_mix(_dot_nt(qi, kin_ref[...]))
    vis_new = lax.broadcasted_iota(I32, score_new.shape, 1) <= lax.broadcasted_iota(I32, score_new.shape, 0)
    key_scr[:, n_keys:] = _ordered_key(score_new, vis_new)
    _topk_bias(key_scr, bias_scr, n_top)
    bias_ref[...] = bias_scr[...]


def _sample_attend_body(group_pages, pt_ref, q_ref, bias_ref, kn_ref, vn_ref, ck_ref, cv_ref, o_ref,
                        kbuf, vbuf, sems, m_scr, l_scr, acc_scr):
    b = pl.program_id(0)
    g = pl.program_id(1)
    nb = pl.num_programs(0)
    ng = pl.num_programs(1)
    gk = kbuf.shape[-1]
    step = b * ng + g
    slot = step % 2
    fetch = lambda bb, gg, sl: _page_copies(pt_ref, bb, gg * group_pages, group_pages, [ck_ref, cv_ref],
                                            [kbuf, vbuf], sems, sl)

    @pl.when(step == 0)
    def _():
        fetch(0, 0, 0)[0]()

    @pl.when(step + 1 < nb * ng)
    def _():
        wrap = g + 1 == ng
        fetch(jnp.where(wrap, b + 1, b), jnp.where(wrap, 0, g + 1), 1 - slot)[0]()

    fetch(b, g, slot)[1]()

    @pl.when(g == 0)
    def _():
        m_scr[...] = jnp.full(m_scr.shape, NEG_BIG, F32)
        l_scr[...] = jnp.zeros(l_scr.shape, F32)
        acc_scr[...] = jnp.zeros(acc_scr.shape, F32)

    q = q_ref[...] * jnp.asarray(HD ** -0.5, BF16)
    per_kv = q.shape[1]
    reps = per_kv // bias_ref.shape[0]

    def update(k_t, v_t, bias):
        bias_g = jnp.concatenate([bias] * reps, axis=0)
        s = jnp.concatenate([_dot(q[n], k_t[n]) + bias_g for n in range(H_KV)], axis=0)
        m_old = m_scr[...]
        m_new = jnp.maximum(m_old, jnp.max(s, axis=1, keepdims=True))
        alpha = jnp.exp(m_old - m_new)
        p = jnp.exp(s - m_new)
        l_scr[...] = alpha * l_scr[...] + jnp.sum(p, axis=1, keepdims=True)
        p16 = p.astype(BF16)
        pv = jnp.concatenate([_dot_nt(p16[n * per_kv:(n + 1) * per_kv], v_t[n]) for n in range(H_KV)], axis=0)
        acc_scr[...] = alpha * acc_scr[...] + pv
        m_scr[...] = m_new

    update(kbuf[slot].astype(BF16), vbuf[slot].astype(BF16),
           bias_ref[:, pl.ds(pl.multiple_of(g * gk, LANES), gk)])

    @pl.when(g == ng - 1)
    def _():
        update(kn_ref[...], vn_ref[...], bias_ref[:, ng * gk:])
        o_ref[...] = acc_scr[...] / l_scr[...]


def _dsa_sample(proj, cache_k, cache_v, cache_kidx, page_table, n_batch, n_q, n_top):
    n_pages = page_table.shape[1]
    n_pool, page = cache_kidx.shape[0], cache_kidx.shape[1]
    n_keys = n_pages * page
    width = n_keys + LANES
    rows = H_A * n_q
    per_q = lambda a, w: a.reshape(n_batch, n_q, w)
    heads_first = lambda a: per_q(a, H_A * HD).reshape(n_batch, n_q, H_A, HD).transpose(0, 2, 1, 3)
    pad_keys = lambda a: jnp.pad(a, ((0, 0), (0, LANES - n_q), (0, 0))).astype(BF16)

    qi = heads_first(proj["qi"]).reshape(n_batch, rows, D_IDX)
    wi = per_q(proj["kw"], LANES)[:, :, D_IDX:D_IDX + H_IDX].transpose(0, 2, 1).reshape(n_batch, rows, 1)
    wi = jnp.broadcast_to(wi, (n_batch, rows, LANES))
    ki_new = pad_keys(per_q(proj["kw"], LANES)[:, :, :D_IDX])
    idx_pages = 16
    bias = pl.pallas_call(
        functools.partial(_sample_index_body, n_top, idx_pages),
        grid_spec=pltpu.PrefetchScalarGridSpec(
            num_scalar_prefetch=1, grid=(n_batch,),
            in_specs=[pl.BlockSpec((None, rows, D_IDX), lambda b, pt: (b, 0, 0)),
                      pl.BlockSpec((None, rows, LANES), lambda b, pt: (b, 0, 0)),
                      pl.BlockSpec((None, LANES, D_IDX), lambda b, pt: (b, 0, 0)),
                      pl.BlockSpec(memory_space=pl.ANY)],
            out_specs=pl.BlockSpec((None, n_q, width), lambda b, pt: (b, 0, 0)),
            scratch_shapes=[pltpu.VMEM((2, D_IDX, n_keys), F32), pltpu.SemaphoreType.DMA((2, 1)),
                            pltpu.VMEM((n_q, width), I32), pltpu.VMEM((n_q, width), F32)]),
        out_shape=jax.ShapeDtypeStruct((n_batch, n_q, width), F32),
        compiler_params=pltpu.CompilerParams(dimension_semantics=("arbitrary",), vmem_limit_bytes=VMEM_LIMIT),
        name="dsa_sample_index",
    )(page_table, qi, wi, ki_new, cache_kidx.transpose(0, 2, 1))

    group = H_A // H_KV
    per_kv = group * n_q
    q_kv = per_q(proj["q"], C_A).reshape(n_batch, n_q, H_KV, group, HD).transpose(0, 2, 3, 1, 4)
    q_kv = q_kv.reshape(n_batch, H_KV, per_kv, HD)
    new_t = lambda a: pad_keys(per_q(a, C_KV)).reshape(n_batch, LANES, H_KV, HD).transpose(0, 2, 3, 1)
    att_pages = min(32, n_pages)
    kv_spec = pl.BlockSpec((None, H_KV, HD, LANES), lambda b, g, pt: (b, 0, 0, 0))
    o = pl.pallas_call(
        functools.partial(_sample_attend_body, att_pages),
        grid_spec=pltpu.PrefetchScalarGridSpec(
            num_scalar_prefetch=1, grid=(n_batch, n_pages // att_pages),
            in_specs=[pl.BlockSpec((None, H_KV, per_kv, HD), lambda b, g, pt: (b, 0, 0, 0)),
                      pl.BlockSpec((None, n_q, width), lambda b, g, pt: (b, 0, 0)),
                      kv_spec, kv_spec,
                      pl.BlockSpec(memory_space=pl.ANY), pl.BlockSpec(memory_space=pl.ANY)],
            out_specs=pl.BlockSpec((None, H_KV * per_kv, HD), lambda b, g, pt: (b, 0, 0)),
            scratch_shapes=[pltpu.VMEM((2, H_KV, HD, att_pages * page), F32),
                            pltpu.VMEM((2, H_KV, HD, att_pages * page), F32),
                            pltpu.SemaphoreType.DMA((2, 2)), pltpu.VMEM((rows, 1), F32), pltpu.VMEM((rows, 1), F32),
                            pltpu.VMEM((rows, HD), F32)]),
        out_shape=jax.ShapeDtypeStruct((n_batch, rows, HD), F32),
        compiler_params=pltpu.CompilerParams(dimension_semantics=("arbitrary", "arbitrary"),
                                             vmem_limit_bytes=VMEM_LIMIT),
        name="dsa_sample_attend",
    )(page_table, q_kv, bias, new_t(proj["k"]), new_t(proj["v"]),
      cache_k.transpose(0, 2, 3, 1), cache_v.transpose(0, 2, 3, 1))
    o = o.reshape(n_batch, H_KV, group, n_q, HD).transpose(0, 3, 1, 2, 4)
    return o.reshape(n_batch * n_q, C_A)


def _round_up(x, m):
    return -(-x // m) * m


def kernel(x_prompt, x_sample, cache_k, cache_v, cache_kidx, state_rwkv, state_rwkv_shift, state_ffn_conv, page_table, meta_tokens, norm1_g, w_in, rwkv_mu, rwkv_w0, rwkv_w_w2, rwkv_a0, rwkv_w_a2, rwkv_w_g2, rwkv_k_k, rwkv_k_a, rwkv_r_k, rwkv_lnx_g, rwkv_lnx_b, w_br_rwkv, w_br_attn, w_out, norm2_g, w_up, conv_w, conv_b, w_down, final_norm_g):
    assert w_in.shape[0] == 1, "single-layer model"
    n_b, seq, d = x_prompt.shape
    n_s, n_q, _ = x_sample.shape
    d_ff = conv_b.shape[-1]
    lp = {"rwkv_mu": rwkv_mu[0], "rwkv_w0": rwkv_w0[0], "rwkv_w_w2": rwkv_w_w2[0], "rwkv_a0": rwkv_a0[0],
          "rwkv_w_a2": rwkv_w_a2[0], "rwkv_w_g2": rwkv_w_g2[0], "rwkv_k_k": rwkv_k_k[0], "rwkv_k_a": rwkv_k_a[0],
          "rwkv_r_k": rwkv_r_k[0], "rwkv_lnx_g": rwkv_lnx_g[0], "rwkv_lnx_b": rwkv_lnx_b[0],
          "w_br_rwkv": w_br_rwkv[0], "w_br_attn": w_br_attn[0], "w_out": w_out[0], "norm2_g": norm2_g[0],
          "w_up": w_up[0], "conv_w": conv_w[0], "conv_b": conv_b[0], "w_down": w_down[0]}
    w_packed = _pack_w_in(w_in[0], d)
    g1 = norm1_g[0].reshape(1, d)

    t_real = seq + N_META
    t_pad = _round_up(t_real, Q_TILE)
    tiles_per_seq = 8
    tm = t_pad // tiles_per_seq
    chunk = 64
    meta = jnp.broadcast_to(meta_tokens[None].astype(x_prompt.dtype), (n_b, N_META, d))
    xp = jnp.concatenate([meta, x_prompt, jnp.zeros((n_b, t_pad - t_real, d), x_prompt.dtype)], axis=1)
    xp = xp.reshape(n_b * t_pad, d)
    proj = _project(xp, g1, w_packed, _rope_tables(jnp.arange(t_pad)), tm)
    yr, s_fin = _rwkv(proj["pr"], jnp.zeros((n_b, P_R), F32), jnp.zeros((n_b, H_R // 2, PAIR, PAIR), F32), lp,
                      n_b, t_pad, t_real, chunk, BF16)
    seq3 = lambda a: a.reshape(n_b, t_pad, a.shape[-1])
    ya = _dsa_prompt(proj, n_b, t_pad, min(TOPK_MAX, seq // 4))
    y_p, conv_p = _merge_ffn(xp, yr, ya, proj["g"], jnp.zeros((n_b, CONV_W - 1, d_ff), F32), lp, final_norm_g,
                             tm, tiles_per_seq, 1, t_real)
    out_p = (seq3(y_p)[:, N_META:t_real],
             seq3(proj["k"])[:, :t_real].reshape(1, n_b, t_real, H_KV, HD),
             seq3(proj["v"])[:, :t_real].reshape(1, n_b, t_real, H_KV, HD),
             seq3(proj["kw"])[:, :t_real, :D_IDX][None],
             _state_from_blockdiag(s_fin)[None],
             seq3(proj["pr"])[:, t_real - 1][None],
             conv_p[None])

    past = page_table.shape[1] * cache_kidx.shape[2]
    xs = x_sample.reshape(n_s * n_q, d)
    tab_s = jnp.tile(_rope_tables(past + jnp.arange(n_q)), (n_s, 1))
    proj_s = _project(xs, g1, w_packed, tab_s, n_s * n_q)
    yr_s, s_fin_s = _rwkv(proj_s["pr"], state_rwkv_shift[0], _state_to_blockdiag(state_rwkv[0]), lp,
                          n_s, n_q, n_q, n_q, F32)
    ya_s = _dsa_sample(proj_s, cache_k[0], cache_v[0], cache_kidx[0], page_table, n_s, n_q,
                       min(TOPK_MAX, (past + n_q) // 4))
    time_major = lambda a: a.reshape(n_s, n_q, a.shape[-1]).transpose(1, 0, 2).reshape(n_q * n_s, a.shape[-1])
    conv_prev_s = state_ffn_conv[0].transpose(1, 0, 2).reshape(1, (CONV_W - 1) * n_s, d_ff)
    y_s, conv_s = _merge_ffn(time_major(xs), time_major(yr_s), time_major(ya_s), time_major(proj_s["g"]),
                             conv_prev_s, lp, final_norm_g, n_s * n_q, 1, n_s, n_q)
    per_q = lambda a: a.reshape(n_s, n_q, a.shape[-1])
    out_s = (y_s.reshape(n_q, n_s, d).transpose(1, 0, 2),
             per_q(proj_s["k"]).reshape(1, n_s, n_q, H_KV, HD),
             per_q(proj_s["v"]).reshape(1, n_s, n_q, H_KV, HD),
             per_q(proj_s["kw"])[:, :, :D_IDX][None],
             _state_from_blockdiag(s_fin_s)[None],
             per_q(proj_s["pr"])[:, n_q - 1][None],
             conv_s.reshape(CONV_W - 1, n_s, d_ff).transpose(1, 0, 2)[None])
    return (out_p[0], out_s[0]) + out_p[1:] + out_s[1:]
```

```python
import functools
import math

import jax
import jax.numpy as jnp
import numpy as np
from jax import lax
from jax.experimental import pallas as pl
from jax.experimental.pallas import tpu as pltpu

F32 = jnp.float32
BF16 = jnp.bfloat16
I32 = jnp.int32

LANES = 128
SUBLANES = 8
VMEM_LIMIT = 56 * 1024 * 1024

N_META = 16
HD = 64
PAIR = 2 * HD
H_R = 8
C_R = H_R * HD
D_W_LORA, D_A_LORA, D_G_LORA = 64, 64, 128
P_R = 3 * C_R + D_W_LORA + D_A_LORA + D_G_LORA
LNX_EPS = 64e-5
H_A, H_KV = 8, 4
C_A, C_KV = H_A * HD, H_KV * HD
H_IDX, D_IDX = 8, 64
TOPK_MAX = 256
ROPE_THETA = 500000.0
ROT = HD // 4
ROT_HALF = ROT // 2
RMS_EPS = 1e-6
CONV_W = 3
Q_TILE = 128
INT_MIN = -(2 ** 31)
NEG_BIG = -1e30

_GROUPS = (("pr", P_R), ("q", C_A), ("qi", H_IDX * D_IDX), ("kd", 2 * C_KV), ("vd", 2 * C_KV),
           ("k", C_KV), ("v", C_KV), ("kw", LANES), ("ki2", LANES), ("g", None))


def _group_offsets(d_model):
    offs, o = {}, 0
    for name, width in _GROUPS:
        width = 2 * d_model if width is None else width
        offs[name] = (o, o + width)
        o += width
    return offs, o


def _pack_w_in(w_in, d_model):
    o = P_R
    q = w_in[:, o:o + C_A]; o += C_A
    k = w_in[:, o:o + C_KV]; o += C_KV
    v = w_in[:, o:o + C_KV]; o += C_KV
    qi = w_in[:, o:o + H_IDX * D_IDX]; o += H_IDX * D_IDX
    wi = w_in[:, o:o + H_IDX]; o += H_IDX
    ki = w_in[:, o:o + D_IDX]; o += D_IDX
    g = w_in[:, o:o + 2 * d_model]
    dup = lambda t: jnp.concatenate([t[:, (n // 2) * HD:(n // 2 + 1) * HD] for n in range(2 * H_KV)], axis=1)
    kw = jnp.concatenate([ki, wi, jnp.zeros((w_in.shape[0], LANES - D_IDX - H_IDX), w_in.dtype)], axis=1)
    packed = jnp.concatenate([w_in[:, :P_R], q, qi, dup(k), dup(v), k, v, kw, jnp.concatenate([ki, ki], axis=1), g],
                             axis=1)
    return packed.astype(BF16)


def _rope_tables(pos):
    inv = ROPE_THETA ** (-jnp.arange(ROT_HALF, dtype=F32) / ROT_HALF)
    ang = pos.astype(F32)[:, None] * inv[None, :]
    cos, sin = jnp.cos(ang), jnp.sin(ang)
    n = pos.shape[0]
    one = jnp.ones((n, HD - ROT), F32)
    zero = jnp.zeros((n, HD - ROT_HALF), F32)
    c64 = jnp.concatenate([cos, cos, one], axis=1)
    s1_64 = jnp.concatenate([-sin, zero], axis=1)
    s2_64 = jnp.concatenate([jnp.zeros((n, ROT_HALF), F32), sin, jnp.zeros((n, HD - ROT), F32)], axis=1)
    wi_scale = jnp.full((n, H_IDX), (H_IDX ** -0.5) * (D_IDX ** -0.5), F32)
    hi_c = jnp.concatenate([wi_scale, jnp.ones((n, HD - H_IDX), F32)], axis=1)
    z64 = jnp.zeros((n, HD), F32)
    return jnp.concatenate([c64, c64, s1_64, s1_64, s2_64, s2_64,
                            c64, hi_c, s1_64, z64, s2_64, z64], axis=1)


def _const_spec(shape):
    nd = len(shape)
    return pl.BlockSpec(shape, lambda *_: (0,) * nd, pipeline_mode=pl.Buffered(1))


def _half_masks(rows):
    lane = lax.broadcasted_iota(I32, (rows, PAIR), 1)
    return lane < HD


def _rms(x, g):
    return x * lax.rsqrt(jnp.mean(x * x, axis=-1, keepdims=True) + RMS_EPS) * g


def _dot(a, b):
    return jnp.dot(a, b, preferred_element_type=F32)


def _dot_nt(a, b):
    return lax.dot_general(a, b, (((1,), (1,)), ((), ())), preferred_element_type=F32)


def _dot_tn(a, b):
    return lax.dot_general(a, b, (((0,), (0,)), ((), ())), preferred_element_type=F32)


def _rope(h, c, s1, s2):
    outs = []
    for j in range(h.shape[1] // LANES):
        hj = h[:, j * LANES:(j + 1) * LANES]
        outs.append(hj * c + pltpu.roll(hj, LANES - ROT_HALF, 1) * s1 + pltpu.roll(hj, ROT_HALF, 1) * s2)
    return outs[0] if len(outs) == 1 else jnp.concatenate(outs, axis=1)


def _proj_body(offs, x_ref, g1_ref, w_ref, tab_ref, pr_o, q_o, qi_o, kd_o, vd_o, k_o, v_o, kw_o, ki2_o, g_o):
    xn = _rms(x_ref[...], g1_ref[...]).astype(BF16)
    mm = lambda name: _dot(xn, w_ref[:, offs[name][0]:offs[name][1]])
    tab = tab_ref[...]
    c, s1, s2 = (tab[:, i * LANES:(i + 1) * LANES] for i in range(3))
    ck, s1k, s2k = (tab[:, i * LANES:(i + 1) * LANES] for i in range(3, 6))
    pr_o[...] = mm("pr")
    g_o[...] = mm("g")
    v_o[...] = mm("v")
    vd_o[...] = mm("vd").astype(BF16)
    q_o[...] = _rope(mm("q"), c, s1, s2).astype(BF16)
    qi_o[...] = _rope(mm("qi"), c, s1, s2).astype(BF16)
    kd_o[...] = _rope(mm("kd"), c, s1, s2).astype(BF16)
    k_o[...] = _rope(mm("k"), c, s1, s2)
    kw_o[...] = _rope(mm("kw"), ck, s1k, s2k)
    ki2_o[...] = _rope(mm("ki2"), c, s1, s2).astype(BF16)


def _project(x, g1, w_packed, tab, tm):
    n, d = x.shape
    offs, n_cols = _group_offsets(d)
    period_tiles = tab.shape[0] // tm
    row = lambda w: pl.BlockSpec((tm, w), lambda i: (i, 0))
    widths = [(name, hi - lo) for name, (lo, hi) in offs.items()]
    dtypes = {"pr": F32, "q": BF16, "qi": BF16, "kd": BF16, "vd": BF16, "k": F32, "v": F32, "kw": F32,
              "ki2": BF16, "g": F32}
    outs = pl.pallas_call(
        functools.partial(_proj_body, offs),
        grid=(n // tm,),
        in_specs=[row(d), _const_spec((1, d)), _const_spec((d, n_cols)),
                  pl.BlockSpec((tm, tab.shape[1]), lambda i: (i % period_tiles, 0))],
        out_specs=[row(w) for _, w in widths],
        out_shape=[jax.ShapeDtypeStruct((n, w), dtypes[name]) for name, w in widths],
        compiler_params=pltpu.CompilerParams(dimension_semantics=("arbitrary",), vmem_limit_bytes=VMEM_LIMIT),
        name="in_proj",
    )(x, g1, w_packed, tab)
    return dict(zip([name for name, _ in widths], outs))


def _head_sum(x, h0):
    outs = []
    for p in range(x.shape[1] // PAIR):
        xp = x[:, p * PAIR:(p + 1) * PAIR]
        s0 = jnp.sum(jnp.where(h0, xp, 0.0), axis=1, keepdims=True)
        s1 = jnp.sum(jnp.where(h0, 0.0, xp), axis=1, keepdims=True)
        outs.append(jnp.where(h0, s0, s1))
    return jnp.concatenate(outs, axis=1)


def _rwkv_prep(chunk, valid, x, prev, mu_ref, w0_ref, ww2_ref, a0_ref, wa2_ref, wg2_ref, kk_ref, ka_ref):
    pm = x + (prev - x) * mu_ref[...]

    r = pm[:, 0:C_R]
    k = pm[:, C_R:2 * C_R]
    v = pm[:, 2 * C_R:3 * C_R]
    o = 3 * C_R
    wd = pm[:, o:o + D_W_LORA]; o += D_W_LORA
    ad = pm[:, o:o + D_A_LORA]; o += D_A_LORA
    gd = pm[:, o:o + D_G_LORA]

    z = -(w0_ref[...] + _dot(jnp.tanh(wd).astype(BF16), ww2_ref[...]))
    softplus = jnp.maximum(z, 0.0) + jnp.log1p(jnp.exp(-jnp.abs(z)))
    logdec = -jnp.exp(-softplus - 0.5)
    gate = jax.nn.sigmoid(a0_ref[...] + _dot(ad.astype(BF16), wa2_ref[...]))
    g_out = _dot(jax.nn.sigmoid(gd).astype(BF16), wg2_ref[...])

    h0 = _half_masks(chunk)
    kk = k * kk_ref[...]
    kk = kk * lax.rsqrt(jnp.maximum(_head_sum(kk * kk, h0), 1e-24))
    k2 = k * (1.0 + (gate - 1.0) * ka_ref[...])

    kk = jnp.where(valid, kk, 0.0)
    k2m = jnp.where(valid, k2, 0.0)
    logdec = jnp.where(valid, logdec, 0.0)

    ri = lax.broadcasted_iota(I32, (chunk, chunk), 0)
    ci = lax.broadcasted_iota(I32, (chunk, chunk), 1)
    tri16 = jnp.where(ri >= ci, 1.0, 0.0).astype(BF16)
    ld_hi = logdec.astype(BF16)
    ld_r = logdec - ld_hi.astype(F32)
    ld_mid = ld_r.astype(BF16)
    ld_lo = (ld_r - ld_mid.astype(F32)).astype(BF16)
    cum = _dot(tri16, ld_hi) + _dot(tri16, ld_mid) + _dot(tri16, ld_lo)
    cum_last = cum[chunk - 1:chunk, :]
    b = kk * gate
    a_t = -kk * jnp.exp(cum - logdec)
    e_neg = jnp.exp(-cum)
    b_t = b * e_neg
    k_t = k2m * e_neg
    r_t = r * jnp.exp(cum)
    e_tail = jnp.exp(cum_last - cum)
    b_g = b * e_tail
    k_g = k2m * e_tail
    g_last = jnp.exp(cum_last)
    return dict(r=r, k2=k2, v=v, g_out=g_out, a_t=a_t, b_t=b_t, k_t=k_t, r_t=r_t, b_g=b_g, k_g=k_g, g_last=g_last)


def _rwkv_body(t_real, chunk, pr_ref, sh_ref, st_ref, mu_ref, w0_ref, ww2_ref, a0_ref, wa2_ref, wg2_ref,
               kk_ref, ka_ref, rk_ref, lg_ref, lb_ref, y_ref, so_ref, s_scr, xs_scr):
    c = pl.program_id(1)
    n_chunks = pl.num_programs(1)
    n_seq = pr_ref.shape[0]
    hdr = SUBLANES

    @pl.when(c == 0)
    def _():
        s_scr[...] = st_ref[...]
        xs_scr[:, hdr - 1:hdr, :] = sh_ref[...]

    @pl.when(c > 0)
    def _():
        xs_scr[:, hdr - 1:hdr, :] = xs_scr[:, hdr + chunk - 1:hdr + chunk, :]

    valid = c * chunk + lax.broadcasted_iota(I32, (chunk, 1), 0) < t_real
    seqs = []
    for i in range(n_seq):
        x = pr_ref[i]
        xs_scr[i, hdr:hdr + chunk, :] = x
        prev = xs_scr[i, hdr - 1:hdr - 1 + chunk, :]
        seqs.append(_rwkv_prep(chunk, valid, x, prev, mu_ref, w0_ref, ww2_ref, a0_ref, wa2_ref, wg2_ref,
                               kk_ref, ka_ref))

    c2 = 2 * chunk
    units = [(i, p) for i in range(n_seq) for p in range(H_R // 2)]
    idx = range(len(units))
    h0 = _half_masks(chunk)
    split = lambda x: jnp.concatenate([jnp.where(h0, x, 0.0), jnp.where(h0, 0.0, x)], axis=0)
    split16 = lambda x: split(x).astype(BF16)
    cols = lambda x, p: x[:, p * PAIR:(p + 1) * PAIR]
    get = lambda name, u: cols(seqs[units[u][0]][name], units[u][1])
    rb = lax.broadcasted_iota(I32, (c2, c2), 0)
    cb = lax.broadcasted_iota(I32, (c2, c2), 1)
    same = (rb < chunk) == (cb < chunk)
    strict = same & (rb > cb)
    incl = same & (rb >= cb)
    eye = (rb == cb).astype(F32)
    n_sq = max(int(math.log2(chunk)) - 1, 0)

    a_s = [split16(get("a_t", u)) for u in idx]
    r_f = [split(get("r_t", u)) for u in idx]
    r_s = [r_f[u].astype(BF16) for u in idx]
    v_s = [split16(get("v", u)) for u in idx]
    bg_s = [split16(get("b_g", u)) for u in idx]
    kg_s = [split16(get("k_g", u)) for u in idx]
    xx = [_dot_nt(jnp.concatenate([a_s[u], r_s[u]], axis=0),
                  jnp.concatenate([split16(get("b_t", u)), split16(get("k_t", u))], axis=0)) for u in idx]
    l_ab = [jnp.where(strict, xx[u][0:c2, 0:c2], 0.0) for u in idx]
    l_ak = [jnp.where(strict, xx[u][0:c2, c2:], 0.0).astype(BF16) for u in idx]
    t_rb = [jnp.where(incl, xx[u][c2:, 0:c2], 0.0).astype(BF16) for u in idx]
    t_rk = [jnp.where(incl, xx[u][c2:, c2:], 0.0).astype(BF16) for u in idx]
    inv = [eye + l_ab[u] for u in idx]
    lp = l_ab
    for _ in range(n_sq):
        lp16 = [lp[u].astype(BF16) for u in idx]
        lp = [_dot(lp16[u], lp16[u]) for u in idx]
        inv = [inv[u] + _dot(inv[u].astype(BF16), lp[u].astype(BF16)) for u in idx]
    inv = [inv[u].astype(BF16) for u in idx]
    a_hat = [_dot(inv[u], a_s[u]).astype(BF16) for u in idx]
    w_s = [_dot(l_ak[u], v_s[u]).astype(BF16) for u in idx]
    u0 = [_dot(inv[u], w_s[u]).astype(BF16) for u in idx]
    r_hat = [(r_f[u] + _dot(t_rb[u], a_hat[u])).astype(BF16) for u in idx]
    y0 = [_dot(t_rb[u], u0[u]) + _dot(t_rk[u], v_s[u]) for u in idx]
    trans = [_dot_tn(a_hat[u], bg_s[u]).astype(BF16) for u in idx]
    add = [_dot_tn(u0[u], bg_s[u]) + _dot_tn(v_s[u], kg_s[u]) for u in idx]
    ys = [[] for _ in range(n_seq)]
    for u, (i, p) in enumerate(units):
        s_old = s_scr[i, p]
        s16 = s_old.astype(BF16)
        y_split = _dot_nt(r_hat[u], s16) + y0[u]
        ys[i].append(y_split[0:chunk] + y_split[chunk:])
        s_scr[i, p] = s_old * get("g_last", u) + _dot(s16, trans[u]) + add[u]

    for i in range(n_seq):
        y = jnp.concatenate(ys[i], axis=1)
        sq = seqs[i]
        mean = _head_sum(y, h0) * (1.0 / HD)
        d = y - mean
        var = _head_sum(d * d, h0) * (1.0 / HD)
        yn = d * lax.rsqrt(var + LNX_EPS) * lg_ref[...] + lb_ref[...]
        yn = yn + _head_sum(sq["r"] * sq["k2"] * rk_ref[...], h0) * sq["v"]
        y_ref[i] = (yn * sq["g_out"]).astype(y_ref.dtype)

    @pl.when(c == n_chunks - 1)
    def _():
        so_ref[...] = s_scr[...]


def _rwkv(pr, shift_prev, state_bd, lp, n_batch, t_pad, t_real, chunk, y_dtype):
    n_chunks = t_pad // chunk
    n_pairs = H_R // 2
    vec = lambda name, w: lp[name].reshape(1, w).astype(F32)
    params = [vec("rwkv_mu", P_R), vec("rwkv_w0", C_R), lp["rwkv_w_w2"].astype(BF16), vec("rwkv_a0", C_R),
              lp["rwkv_w_a2"].astype(BF16), lp["rwkv_w_g2"].astype(BF16), vec("rwkv_k_k", C_R),
              vec("rwkv_k_a", C_R), vec("rwkv_r_k", C_R), vec("rwkv_lnx_g", C_R), vec("rwkv_lnx_b", C_R)]
    n_seq = next(n for n in (4, 2, 1) if n_batch % n == 0)
    y, s_out = pl.pallas_call(
        functools.partial(_rwkv_body, t_real, chunk),
        grid=(n_batch // n_seq, n_chunks),
        in_specs=[pl.BlockSpec((n_seq, chunk, P_R), lambda b, c: (b, c, 0)),
                  pl.BlockSpec((n_seq, 1, P_R), lambda b, c: (b, 0, 0)),
                  pl.BlockSpec((n_seq, n_pairs, PAIR, PAIR), lambda b, c: (b, 0, 0, 0))]
                 + [_const_spec(p.shape) for p in params],
        out_specs=[pl.BlockSpec((n_seq, chunk, C_R), lambda b, c: (b, c, 0)),
                   pl.BlockSpec((n_seq, n_pairs, PAIR, PAIR), lambda b, c: (b, 0, 0, 0))],
        out_shape=[jax.ShapeDtypeStruct((n_batch, t_pad, C_R), y_dtype),
                   jax.ShapeDtypeStruct((n_batch, n_pairs, PAIR, PAIR), F32)],
        scratch_shapes=[pltpu.VMEM((n_seq, n_pairs, PAIR, PAIR), F32),
                        pltpu.VMEM((n_seq, SUBLANES + chunk, P_R), F32)],
        compiler_params=pltpu.CompilerParams(dimension_semantics=("arbitrary", "arbitrary"),
                                             vmem_limit_bytes=VMEM_LIMIT),
        name="rwkv7_chunked",
    )(pr.reshape(n_batch, t_pad, P_R), shift_prev.reshape(n_batch, 1, P_R), state_bd, *params)
    return y.reshape(n_batch * t_pad, C_R), s_out


def _state_to_blockdiag(s):
    b = s.shape[0]
    s = s.reshape(b, H_R // 2, 2, HD, HD)
    z = jnp.zeros_like(s[:, :, 0])
    top = jnp.concatenate([s[:, :, 0], z], axis=-1)
    bot = jnp.concatenate([z, s[:, :, 1]], axis=-1)
    return jnp.concatenate([top, bot], axis=-2)


def _state_from_blockdiag(s):
    b = s.shape[0]
    return jnp.stack([s[:, :, :HD, :HD], s[:, :, HD:, HD:]], axis=2).reshape(b, H_R, HD, HD)


def _ordered_key(score, visible):
    bits = pltpu.bitcast(score + 0.0, I32)
    key = jnp.where(bits < 0, bits ^ jnp.int32(0x7FFFFFFF), bits)
    return jnp.where(visible, key, jnp.int32(INT_MIN))


def _kth_largest(count_ge, n_bits, k, rows, width):
    def step(it, carry):
        prefix, n_ge = carry
        trial = prefix | jnp.left_shift(jnp.int32(1), jnp.int32(n_bits - 1) - it)
        cnt = count_ge(trial)
        take = cnt >= k
        return jnp.where(take, trial, prefix), jnp.where(take, cnt, n_ge)

    return lax.fori_loop(0, n_bits, step, (jnp.zeros((rows, 1), I32), jnp.full((rows, 1), width, F32)), unroll=4)


def _count16(ref16, hit):
    acc = None
    for j in range(ref16.shape[1] // LANES):
        one = jnp.where(hit(ref16[:, j * LANES:(j + 1) * LANES]), jnp.int16(1), jnp.int16(0))
        acc = one if acc is None else acc + one
    return jnp.sum(acc.astype(F32), axis=1, keepdims=True)


def _topk_bias(key_scr, bias_scr, n_top, half_scr=None):
    rows, width = key_scr.shape
    k_f = jnp.float32(n_top)
    n_parts = max(1, min(width // LANES, (8 * SUBLANES) // rows))
    bounds = [round(j * (width // LANES) / n_parts) * LANES for j in range(n_parts + 1)]

    def count_ge(thr):
        parts = [jnp.sum(jnp.where(key_scr[:, lo:hi] >= thr, 1.0, 0.0), axis=1, keepdims=True)
                 for lo, hi in zip(bounds[:-1], bounds[1:])]
        return functools.reduce(lambda a, b: a + b, parts)

    if half_scr is None:
        prefix, n_ge = _kth_largest(lambda t: count_ge(t ^ jnp.int32(INT_MIN)), 32, k_f, rows, width)
        thr = prefix ^ jnp.int32(INT_MIN)
    else:
        hi_scr, lo_scr = half_scr
        bias16 = 1 << 15
        signed16 = lambda biased: (biased - bias16).astype(jnp.int16)
        key = key_scr[...]
        hi_scr[...] = lax.shift_right_arithmetic(key, 16).astype(jnp.int16)
        top, _ = _kth_largest(lambda t: _count16(hi_scr, lambda blk: blk >= signed16(t)), 16, k_f, rows, width)
        top16 = signed16(top)
        n_above = _count16(hi_scr, lambda blk: blk > top16)
        low = ((key & jnp.int32(0xFFFF)) - bias16).astype(jnp.int16)
        lo_scr[...] = jnp.where(hi_scr[...] == top16, low, jnp.int16(-bias16))
        bottom, _ = _kth_largest(lambda t: _count16(lo_scr, lambda blk: blk >= signed16(t)), 16, k_f - n_above,
                                 rows, width)
        thr = jnp.left_shift(top - bias16, 16) | bottom
        n_ge = count_ge(thr)
    bias_scr[...] = jnp.where(key_scr[...] >= jnp.maximum(thr, jnp.int32(INT_MIN + 1)), 0.0, -jnp.inf)
    ambiguous = (n_ge > k_f) & (thr > jnp.int32(INT_MIN))

    @pl.when(jnp.max(jnp.where(ambiguous, 1.0, 0.0)) > 0.0)
    def _():
        upper = (lax.broadcasted_iota(I32, (LANES, LANES), 0)
                 < lax.broadcasted_iota(I32, (LANES, LANES), 1)).astype(BF16)
        n_gt = jnp.sum(jnp.where(key_scr[...] > thr, 1.0, 0.0), axis=1, keepdims=True)
        need = k_f - n_gt

        def block(kb, seen):
            lo = pl.multiple_of(kb * LANES, LANES)
            kblk = key_scr[:, pl.ds(lo, LANES)]
            eq = kblk == thr
            eq16 = jnp.where(eq, 1.0, 0.0).astype(BF16)
            rank = seen + _dot(eq16, upper)
            take = (kblk > thr) | (eq & (rank < need))
            bias_scr[:, pl.ds(lo, LANES)] = jnp.where(take & (kblk > jnp.int32(INT_MIN)), 0.0, -jnp.inf)
            return seen + jnp.sum(jnp.where(eq, 1.0, 0.0), axis=1, keepdims=True)

        lax.fori_loop(0, width // LANES, block, jnp.zeros((rows, 1), F32))


def _indexer_scores(qi_ref, ki2, wi, h0):
    rows = qi_ref.shape[0]
    score = None
    for p in range(H_IDX // 2):
        qp = qi_ref[:, p * PAIR:(p + 1) * PAIR]
        zero = jnp.zeros_like(qp)
        lhs = jnp.concatenate([jnp.where(h0, qp, zero), jnp.where(h0, zero, qp)], axis=0)
        sc = jnp.maximum(_dot_nt(lhs, ki2), 0.0)
        part = wi[:, 2 * p:2 * p + 1] * sc[0:rows] + wi[:, 2 * p + 1:2 * p + 2] * sc[rows:]
        score = part if score is None else score + part
    return score


def _dsa_prompt_tile(n_top, t_keys, i, q_ref, qi_ref, kw_ref, ki2_ref, kd_ref, vd_ref, o_ref, key_scr, bias_scr,
                     hi_scr, lo_scr):
    n_seq, tq = q_ref.shape[0], q_ref.shape[1]
    h0 = _half_masks(tq)
    q_pos = i * tq + lax.broadcasted_iota(I32, (tq, 1), 0)
    visible = lax.broadcasted_iota(I32, (tq, t_keys), 1) <= q_pos
    key_scr = key_scr.at[:, 0:t_keys]
    bias_scr = bias_scr.at[:, 0:t_keys]

    seq_rows = lambda j: pl.ds(pl.multiple_of(j * tq, tq), tq)

    @pl.loop(0, n_seq)
    def _(j):
        wi = kw_ref[j, :, D_IDX:D_IDX + H_IDX]
        score = _indexer_scores(qi_ref.at[j], ki2_ref[j, 0:t_keys, :], wi, h0)
        key_scr[seq_rows(j), :] = _ordered_key(score, visible)

    _topk_bias(key_scr, bias_scr, n_top, (hi_scr.at[:, 0:t_keys], lo_scr.at[:, 0:t_keys]))

    scale = jnp.asarray(HD ** -0.5, BF16)

    @pl.loop(0, n_seq)
    def _(j):
        bias = bias_scr[seq_rows(j), :]
        bias2 = jnp.concatenate([bias, bias], axis=0)
        for n in range(H_KV):
            sl = slice(n * PAIR, (n + 1) * PAIR)
            qp = q_ref[j, :, sl] * scale
            zero = jnp.zeros_like(qp)
            lhs = jnp.concatenate([jnp.where(h0, qp, zero), jnp.where(h0, zero, qp)], axis=0)
            s = _dot_nt(lhs, kd_ref[j, 0:t_keys, sl]) + bias2
            m = jnp.max(s, axis=1, keepdims=True)
            p = jnp.exp(s - m)
            l = jnp.sum(p, axis=1, keepdims=True)
            o = _dot(p.astype(BF16), vd_ref[j, 0:t_keys, sl]) / l
            o_ref[j, :, sl] = jnp.where(h0, o[0:tq], o[tq:]).astype(o_ref.dtype)


def _dsa_prompt_body(n_top, n_buckets, *refs):
    tq = refs[0].shape[1]
    n_tiles = refs[3].shape[1] // tq
    i = pl.program_id(1)
    edges = [round(j * n_tiles / n_buckets) for j in range(n_buckets + 1)]
    for lo, hi in zip(edges[:-1], edges[1:]):
        if hi > lo:
            pl.when((i >= lo) & (i < hi))(functools.partial(_dsa_prompt_tile, n_top, hi * tq, i, *refs))


def _dsa_prompt(proj, n_batch, t_pad, n_top):
    nq = t_pad // Q_TILE
    n_seq = 2 if n_batch % 2 == 0 else 1
    rows = n_seq * Q_TILE
    qrow = lambda w: pl.BlockSpec((n_seq, Q_TILE, w), lambda b, i: (b, i, 0))
    seq = lambda w: pl.BlockSpec((n_seq, t_pad, w), lambda b, i: (b, 0, 0))
    per_seq = lambda name: proj[name].reshape(n_batch, t_pad, proj[name].shape[-1])
    ya = pl.pallas_call(
        functools.partial(_dsa_prompt_body, n_top, 3),
        grid=(n_batch // n_seq, nq),
        in_specs=[qrow(C_A), qrow(H_IDX * D_IDX), qrow(LANES), seq(LANES), seq(2 * C_KV), seq(2 * C_KV)],
        out_specs=qrow(C_A),
        out_shape=jax.ShapeDtypeStruct((n_batch, t_pad, C_A), BF16),
        scratch_shapes=[pltpu.VMEM((rows, t_pad), I32), pltpu.VMEM((rows, t_pad), F32),
                        pltpu.VMEM((rows, t_pad), jnp.int16), pltpu.VMEM((rows, t_pad), jnp.int16)],
        compiler_params=pltpu.CompilerParams(dimension_semantics=("arbitrary", "arbitrary"),
                                             vmem_limit_bytes=VMEM_LIMIT),
        name="dsa_prompt",
    )(*[per_seq(name) for name in ("q", "qi", "kw", "ki2", "kd", "vd")])
    return ya.reshape(n_batch * t_pad, C_A)


def _ffn_body(tiles_per_seq, stride, last_tile, last_lo, x_ref, yr_ref, ya_ref, g_ref, cp_ref, wbr_ref, wba_ref,
              wo_ref, g2_ref, wup_ref, cw_ref, cb_ref, wdn_ref, gf_ref, y_ref, cl_ref, a_scr):
    i = pl.program_id(0)
    tm, d = x_ref.shape
    d_ff = cb_ref.shape[1]
    hdr = max(SUBLANES, 2 * stride)

    @pl.when(i % tiles_per_seq == 0)
    def _():
        a_scr[hdr - 2 * stride:hdr, :] = cp_ref[...]

    @pl.when(i % tiles_per_seq != 0)
    def _():
        a_scr[hdr - 2 * stride:hdr, :] = a_scr[hdr + tm - 2 * stride:hdr + tm, :]

    g = g_ref[...]
    merged = (jax.nn.sigmoid(g[:, :d]) * _dot(yr_ref[...].astype(BF16), wbr_ref[...])
              + jax.nn.sigmoid(g[:, d:]) * _dot(ya_ref[...].astype(BF16), wba_ref[...]))
    x1 = x_ref[...] + _dot(merged.astype(BF16), wo_ref[...])
    hn = _rms(x1, g2_ref[...]).astype(BF16)
    a_scr[hdr:hdr + tm, :] = _dot(hn, wup_ref[:, :d_ff])
    gate = _dot(hn, wup_ref[:, d_ff:])
    cw = cw_ref[...]
    conv = cb_ref[...] + a_scr[hdr - 2 * stride:hdr - 2 * stride + tm, :] * cw[0:1]
    conv = conv + a_scr[hdr - stride:hdr - stride + tm, :] * cw[1:2]
    conv = conv + a_scr[hdr:hdr + tm, :] * cw[2:3]
    act = 0.5 * conv * (1.0 + lax.erf(conv * (2.0 ** -0.5)))
    x2 = x1 + _dot((act * gate).astype(BF16), wdn_ref[...])
    y_ref[...] = _rms(x2, gf_ref[...])

    @pl.when(i % tiles_per_seq == last_tile)
    def _():
        cl_ref[...] = a_scr[hdr + last_lo:hdr + last_lo + 2 * stride, :]


def _merge_ffn(x, yr, ya, g, conv_prev, lp, gf, tm, tiles_per_seq, stride, t_real):
    n, d = x.shape
    d_ff = lp["conv_b"].shape[-1]
    n_seq = n // (tm * tiles_per_seq)
    first_last = (t_real - 2) * stride
    last_tile, last_lo = first_last // tm, first_last % tm
    hdr = max(SUBLANES, 2 * stride)
    row = lambda w: pl.BlockSpec((tm, w), lambda i: (i, 0))
    weights = [lp["w_br_rwkv"].astype(BF16), lp["w_br_attn"].astype(BF16), lp["w_out"].astype(BF16),
               lp["norm2_g"].reshape(1, d), lp["w_up"].astype(BF16), lp["conv_w"], lp["conv_b"].reshape(1, d_ff),
               lp["w_down"].astype(BF16), gf.reshape(1, d)]
    y, conv_last = pl.pallas_call(
        functools.partial(_ffn_body, tiles_per_seq, stride, last_tile, last_lo),
        grid=(n // tm,),
        in_specs=[row(d), row(C_R), row(C_A), row(2 * d),
                  pl.BlockSpec((None, 2 * stride, d_ff), lambda i: (i // tiles_per_seq, 0, 0))]
                 + [_const_spec(w.shape) for w in weights],
        out_specs=[row(d), pl.BlockSpec((None, 2 * stride, d_ff), lambda i: (i // tiles_per_seq, 0, 0))],
        out_shape=[jax.ShapeDtypeStruct((n, d), F32), jax.ShapeDtypeStruct((n_seq, 2 * stride, d_ff), F32)],
        scratch_shapes=[pltpu.VMEM((hdr + tm, d_ff), F32)],
        compiler_params=pltpu.CompilerParams(dimension_semantics=("arbitrary",), vmem_limit_bytes=VMEM_LIMIT),
        name="merge_convffn",
    )(x, yr, ya, g, conv_prev, *weights)
    return y, conv_last


def _page_copies(pt_ref, batch, first_page, n_pages, srcs, dsts, sems, slot):
    def copies(pg):
        page = pt_ref[batch, first_page + pg]
        rows = srcs[0].shape[-1]
        lanes = pl.ds(pl.multiple_of(pg * rows, rows), rows)
        window = lambda dst: dst.at[(slot,) + (slice(None),) * (len(dst.shape) - 2) + (lanes,)]
        return [pltpu.make_async_copy(src.at[page], window(dst), sems.at[slot, a])
                for a, (src, dst) in enumerate(zip(srcs, dsts))]

    def start():
        def one(pg, carry):
            for cp in copies(pg):
                cp.start()
            return carry
        lax.fori_loop(0, n_pages, one, 0)

    def wait():
        def one(pg, carry):
            for cp in copies(pg):
                cp.wait()
            return carry
        lax.fori_loop(0, n_pages, one, 0)

    return start, wait


def _sample_index_body(n_top, group_pages, pt_ref, qi_ref, w_ref, kin_ref, cache_ref, bias_ref,
                       kbuf, sems, key_scr, bias_scr):
    b = pl.program_id(0)
    nb = pl.num_programs(0)
    page = cache_ref.shape[-1]
    n_keys = kbuf.shape[-1]
    n_pages = n_keys // page
    n_q = key_scr.shape[0]
    slot = b % 2
    fetch = lambda bb, sl: _page_copies(pt_ref, bb, 0, n_pages, [cache_ref], [kbuf], sems, sl)

    @pl.when(b == 0)
    def _():
        fetch(0, 0)[0]()

    @pl.when(b + 1 < nb)
    def _():
        fetch(b + 1, 1 - slot)[0]()

    fetch(b, slot)[1]()

    qi = qi_ref[...]
    w = w_ref[:, 0:1]

    def head_mix(sc):
        sc = jnp.maximum(sc, 0.0) * w
        out = sc[0:n_q]
        for h in range(1, H_IDX):
            out = out + sc[h * n_q:(h + 1) * n_q]
        return out

    gk = group_pages * page
    for g in range(n_pages // group_pages):
        ki_t = kbuf[slot, :, g * gk:(g + 1) * gk].astype(BF16)
        score = head_mix(_dot(qi, ki_t))
        key_scr[:, g * gk:(g + 1) * gk] = _ordered_key(score, jnp.full(score.shape, True))
    score_new = head_mix(_dot_nt(qi, kin_ref[...]))
    vis_new = lax.broadcasted_iota(I32, score_new.shape, 1) <= lax.broadcasted_iota(I32, score_new.shape, 0)
    key_scr[:, n_keys:] = _ordered_key(score_new, vis_new)
    _topk_bias(key_scr, bias_scr, n_top)
    bias_ref[...] = bias_scr[...]


def _sample_attend_body(group_pages, pt_ref, q_ref, bias_ref, kn_ref, vn_ref, ck_ref, cv_ref, o_ref,
                        kbuf, vbuf, sems, m_scr, l_scr, acc_scr):
    b = pl.program_id(0)
    g = pl.program_id(1)
    nb = pl.num_programs(0)
    ng = pl.num_programs(1)
    gk = kbuf.shape[-1]
    step = b * ng + g
    slot = step % 2
    fetch = lambda bb, gg, sl: _page_copies(pt_ref, bb, gg * group_pages, group_pages, [ck_ref, cv_ref],
                                            [kbuf, vbuf], sems, sl)

    @pl.when(step == 0)
    def _():
        fetch(0, 0, 0)[0]()

    @pl.when(step + 1 < nb * ng)
    def _():
        wrap = g + 1 == ng
        fetch(jnp.where(wrap, b + 1, b), jnp.where(wrap, 0, g + 1), 1 - slot)[0]()

    fetch(b, g, slot)[1]()

    @pl.when(g == 0)
    def _():
        m_scr[...] = jnp.full(m_scr.shape, NEG_BIG, F32)
        l_scr[...] = jnp.zeros(l_scr.shape, F32)
        acc_scr[...] = jnp.zeros(acc_scr.shape, F32)

    q = q_ref[...] * jnp.asarray(HD ** -0.5, BF16)
    per_kv = q.shape[1]
    reps = per_kv // bias_ref.shape[0]

    def update(k_t, v_t, bias):
        bias_g = jnp.concatenate([bias] * reps, axis=0)
        s = jnp.concatenate([_dot(q[n], k_t[n]) + bias_g for n in range(H_KV)], axis=0)
        m_old = m_scr[...]
        m_new = jnp.maximum(m_old, jnp.max(s, axis=1, keepdims=True))
        alpha = jnp.exp(m_old - m_new)
        p = jnp.exp(s - m_new)
        l_scr[...] = alpha * l_scr[...] + jnp.sum(p, axis=1, keepdims=True)
        p16 = p.astype(BF16)
        pv = jnp.concatenate([_dot_nt(p16[n * per_kv:(n + 1) * per_kv], v_t[n]) for n in range(H_KV)], axis=0)
        acc_scr[...] = alpha * acc_scr[...] + pv
        m_scr[...] = m_new

    update(kbuf[slot].astype(BF16), vbuf[slot].astype(BF16),
           bias_ref[:, pl.ds(pl.multiple_of(g * gk, LANES), gk)])

    @pl.when(g == ng - 1)
    def _():
        update(kn_ref[...], vn_ref[...], bias_ref[:, ng * gk:])
        o_ref[...] = acc_scr[...] / l_scr[...]


def _dsa_sample(proj, cache_k, cache_v, cache_kidx, page_table, n_batch, n_q, n_top):
    n_pages = page_table.shape[1]
    n_pool, page = cache_kidx.shape[0], cache_kidx.shape[1]
    n_keys = n_pages * page
    width = n_keys + LANES
    rows = H_A * n_q
    per_q = lambda a, w: a.reshape(n_batch, n_q, w)
    heads_first = lambda a: per_q(a, H_A * HD).reshape(n_batch, n_q, H_A, HD).transpose(0, 2, 1, 3)
    pad_keys = lambda a: jnp.pad(a, ((0, 0), (0, LANES - n_q), (0, 0))).astype(BF16)

    qi = heads_first(proj["qi"]).reshape(n_batch, rows, D_IDX)
    wi = per_q(proj["kw"], LANES)[:, :, D_IDX:D_IDX + H_IDX].transpose(0, 2, 1).reshape(n_batch, rows, 1)
    wi = jnp.broadcast_to(wi, (n_batch, rows, LANES))
    ki_new = pad_keys(per_q(proj["kw"], LANES)[:, :, :D_IDX])
    idx_pages = 16
    bias = pl.pallas_call(
        functools.partial(_sample_index_body, n_top, idx_pages),
        grid_spec=pltpu.PrefetchScalarGridSpec(
            num_scalar_prefetch=1, grid=(n_batch,),
            in_specs=[pl.BlockSpec((None, rows, D_IDX), lambda b, pt: (b, 0, 0)),
                      pl.BlockSpec((None, rows, LANES), lambda b, pt: (b, 0, 0)),
                      pl.BlockSpec((None, LANES, D_IDX), lambda b, pt: (b, 0, 0)),
                      pl.BlockSpec(memory_space=pl.ANY)],
            out_specs=pl.BlockSpec((None, n_q, width), lambda b, pt: (b, 0, 0)),
            scratch_shapes=[pltpu.VMEM((2, D_IDX, n_keys), F32), pltpu.SemaphoreType.DMA((2, 1)),
                            pltpu.VMEM((n_q, width), I32), pltpu.VMEM((n_q, width), F32)]),
        out_shape=jax.ShapeDtypeStruct((n_batch, n_q, width), F32),
        compiler_params=pltpu.CompilerParams(dimension_semantics=("arbitrary",), vmem_limit_bytes=VMEM_LIMIT),
        name="dsa_sample_index",
    )(page_table, qi, wi, ki_new, cache_kidx.transpose(0, 2, 1))

    group = H_A // H_KV
    per_kv = group * n_q
    q_kv = per_q(proj["q"], C_A).reshape(n_batch, n_q, H_KV, group, HD).transpose(0, 2, 3, 1, 4)
    q_kv = q_kv.reshape(n_batch, H_KV, per_kv, HD)
    new_t = lambda a: pad_keys(per_q(a, C_KV)).reshape(n_batch, LANES, H_KV, HD).transpose(0, 2, 3, 1)
    att_pages = min(32, n_pages)
    kv_spec = pl.BlockSpec((None, H_KV, HD, LANES), lambda b, g, pt: (b, 0, 0, 0))
    o = pl.pallas_call(
        functools.partial(_sample_attend_body, att_pages),
        grid_spec=pltpu.PrefetchScalarGridSpec(
            num_scalar_prefetch=1, grid=(n_batch, n_pages // att_pages),
            in_specs=[pl.BlockSpec((None, H_KV, per_kv, HD), lambda b, g, pt: (b, 0, 0, 0)),
                      pl.BlockSpec((None, n_q, width), lambda b, g, pt: (b, 0, 0)),
                      kv_spec, kv_spec,
                      pl.BlockSpec(memory_space=pl.ANY), pl.BlockSpec(memory_space=pl.ANY)],
            out_specs=pl.BlockSpec((None, H_KV * per_kv, HD), lambda b, g, pt: (b, 0, 0)),
            scratch_shapes=[pltpu.VMEM((2, H_KV, HD, att_pages * page), F32),
                            pltpu.VMEM((2, H_KV, HD, att_pages * page), F32),
                            pltpu.SemaphoreType.DMA((2, 2)), pltpu.VMEM((rows, 1), F32), pltpu.VMEM((rows, 1), F32),
                            pltpu.VMEM((rows, HD), F32)]),
        out_shape=jax.ShapeDtypeStruct((n_batch, rows, HD), F32),
        compiler_params=pltpu.CompilerParams(dimension_semantics=("arbitrary", "arbitrary"),
                                             vmem_limit_bytes=VMEM_LIMIT),
        name="dsa_sample_attend",
    )(page_table, q_kv, bias, new_t(proj["k"]), new_t(proj["v"]),
      cache_k.transpose(0, 2, 3, 1), cache_v.transpose(0, 2, 3, 1))
    o = o.reshape(n_batch, H_KV, group, n_q, HD).transpose(0, 3, 1, 2, 4)
    return o.reshape(n_batch * n_q, C_A)


def _round_up(x, m):
    return -(-x // m) * m


def kernel(x_prompt, x_sample, cache_k, cache_v, cache_kidx, state_rwkv, state_rwkv_shift, state_ffn_conv, page_table, meta_tokens, norm1_g, w_in, rwkv_mu, rwkv_w0, rwkv_w_w2, rwkv_a0, rwkv_w_a2, rwkv_w_g2, rwkv_k_k, rwkv_k_a, rwkv_r_k, rwkv_lnx_g, rwkv_lnx_b, w_br_rwkv, w_br_attn, w_out, norm2_g, w_up, conv_w, conv_b, w_down, final_norm_g):
    assert w_in.shape[0] == 1, "single-layer model"
    n_b, seq, d = x_prompt.shape
    n_s, n_q, _ = x_sample.shape
    d_ff = conv_b.shape[-1]
    lp = {"rwkv_mu": rwkv_mu[0], "rwkv_w0": rwkv_w0[0], "rwkv_w_w2": rwkv_w_w2[0], "rwkv_a0": rwkv_a0[0],
          "rwkv_w_a2": rwkv_w_a2[0], "rwkv_w_g2": rwkv_w_g2[0], "rwkv_k_k": rwkv_k_k[0], "rwkv_k_a": rwkv_k_a[0],
          "rwkv_r_k": rwkv_r_k[0], "rwkv_lnx_g": rwkv_lnx_g[0], "rwkv_lnx_b": rwkv_lnx_b[0],
          "w_br_rwkv": w_br_rwkv[0], "w_br_attn": w_br_attn[0], "w_out": w_out[0], "norm2_g": norm2_g[0],
          "w_up": w_up[0], "conv_w": conv_w[0], "conv_b": conv_b[0], "w_down": w_down[0]}
    w_packed = _pack_w_in(w_in[0], d)
    g1 = norm1_g[0].reshape(1, d)

    t_real = seq + N_META
    t_pad = _round_up(t_real, Q_TILE)
    tiles_per_seq = 8
    tm = t_pad // tiles_per_seq
    chunk = 64
    meta = jnp.broadcast_to(meta_tokens[None].astype(x_prompt.dtype), (n_b, N_META, d))
    xp = jnp.concatenate([meta, x_prompt, jnp.zeros((n_b, t_pad - t_real, d), x_prompt.dtype)], axis=1)
    xp = xp.reshape(n_b * t_pad, d)
    proj = _project(xp, g1, w_packed, _rope_tables(jnp.arange(t_pad)), tm)
    yr, s_fin = _rwkv(proj["pr"], jnp.zeros((n_b, P_R), F32), jnp.zeros((n_b, H_R // 2, PAIR, PAIR), F32), lp,
                      n_b, t_pad, t_real, chunk, BF16)
    seq3 = lambda a: a.reshape(n_b, t_pad, a.shape[-1])
    ya = _dsa_prompt(proj, n_b, t_pad, min(TOPK_MAX, seq // 4))
    y_p, conv_p = _merge_ffn(xp, yr, ya, proj["g"], jnp.zeros((n_b, CONV_W - 1, d_ff), F32), lp, final_norm_g,
                             tm, tiles_per_seq, 1, t_real)
    out_p = (seq3(y_p)[:, N_META:t_real],
             seq3(proj["k"])[:, :t_real].reshape(1, n_b, t_real, H_KV, HD),
             seq3(proj["v"])[:, :t_real].reshape(1, n_b, t_real, H_KV, HD),
             seq3(proj["kw"])[:, :t_real, :D_IDX][None],
             _state_from_blockdiag(s_fin)[None],
             seq3(proj["pr"])[:, t_real - 1][None],
             conv_p[None])

    past = page_table.shape[1] * cache_kidx.shape[2]
    xs = x_sample.reshape(n_s * n_q, d)
    tab_s = jnp.tile(_rope_tables(past + jnp.arange(n_q)), (n_s, 1))
    proj_s = _project(xs, g1, w_packed, tab_s, n_s * n_q)
    yr_s, s_fin_s = _rwkv(proj_s["pr"], state_rwkv_shift[0], _state_to_blockdiag(state_rwkv[0]), lp,
                          n_s, n_q, n_q, n_q, F32)
    ya_s = _dsa_sample(proj_s, cache_k[0], cache_v[0], cache_kidx[0], page_table, n_s, n_q,
                       min(TOPK_MAX, (past + n_q) // 4))
    time_major = lambda a: a.reshape(n_s, n_q, a.shape[-1]).transpose(1, 0, 2).reshape(n_q * n_s, a.shape[-1])
    conv_prev_s = state_ffn_conv[0].transpose(1, 0, 2).reshape(1, (CONV_W - 1) * n_s, d_ff)
    y_s, conv_s = _merge_ffn(time_major(xs), time_major(yr_s), time_major(ya_s), time_major(proj_s["g"]),
                             conv_prev_s, lp, final_norm_g, n_s * n_q, 1, n_s, n_q)
    per_q = lambda a: a.reshape(n_s, n_q, a.shape[-1])
    out_s = (y_s.reshape(n_q, n_s, d).transpose(1, 0, 2),
             per_q(proj_s["k"]).reshape(1, n_s, n_q, H_KV, HD),
             per_q(proj_s["v"]).reshape(1, n_s, n_q, H_KV, HD),
             per_q(proj_s["kw"])[:, :, :D_IDX][None],
             _state_from_blockdiag(s_fin_s)[None],
             per_q(proj_s["pr"])[:, n_q - 1][None],
             conv_s.reshape(CONV_W - 1, n_s, d_ff).transpose(1, 0, 2)[None])
    return (out_p[0], out_s[0]) + out_p[1:] + out_s[1:]
```

```python
import functools
import math

import jax
import jax.numpy as jnp
import numpy as np
from jax import lax
from jax.experimental import pallas as pl
from jax.experimental.pallas import tpu as pltpu

F32 = jnp.float32
BF16 = jnp.bfloat16
I32 = jnp.int32

LANES = 128
SUBLANES = 8
VMEM_LIMIT = 56 * 1024 * 1024

N_META = 16
HD = 64
PAIR = 2 * HD
H_R = 8
C_R = H_R * HD
D_W_LORA, D_A_LORA, D_G_LORA = 64, 64, 128
P_R = 3 * C_R + D_W_LORA + D_A_LORA + D_G_LORA
LNX_EPS = 64e-5
H_A, H_KV = 8, 4
C_A, C_KV = H_A * HD, H_KV * HD
H_IDX, D_IDX = 8, 64
TOPK_MAX = 256
ROPE_THETA = 500000.0
ROT = HD // 4
ROT_HALF = ROT // 2
RMS_EPS = 1e-6
CONV_W = 3
Q_TILE = 128
INT_MIN = -(2 ** 31)
NEG_BIG = -1e30

_GROUPS = (("pr", P_R), ("q", C_A), ("qi", H_IDX * D_IDX), ("kd", 2 * C_KV), ("vd", 2 * C_KV),
           ("k", C_KV), ("v", C_KV), ("kw", LANES), ("ki2", LANES), ("g", None))


def _group_offsets(d_model):
    offs, o = {}, 0
    for name, width in _GROUPS:
        width = 2 * d_model if width is None else width
        offs[name] = (o, o + width)
        o += width
    return offs, o


def _pack_w_in(w_in, d_model):
    o = P_R
    q = w_in[:, o:o + C_A]; o += C_A
    k = w_in[:, o:o + C_KV]; o += C_KV
    v = w_in[:, o:o + C_KV]; o += C_KV
    qi = w_in[:, o:o + H_IDX * D_IDX]; o += H_IDX * D_IDX
    wi = w_in[:, o:o + H_IDX]; o += H_IDX
    ki = w_in[:, o:o + D_IDX]; o += D_IDX
    g = w_in[:, o:o + 2 * d_model]
    dup = lambda t: jnp.concatenate([t[:, (n // 2) * HD:(n // 2 + 1) * HD] for n in range(2 * H_KV)], axis=1)
    kw = jnp.concatenate([ki, wi, jnp.zeros((w_in.shape[0], LANES - D_IDX - H_IDX), w_in.dtype)], axis=1)
    packed = jnp.concatenate([w_in[:, :P_R], q, qi, dup(k), dup(v), k, v, kw, jnp.concatenate([ki, ki], axis=1), g],
                             axis=1)
    return packed.astype(BF16)


def _rope_tables(pos):
    inv = ROPE_THETA ** (-jnp.arange(ROT_HALF, dtype=F32) / ROT_HALF)
    ang = pos.astype(F32)[:, None] * inv[None, :]
    cos, sin = jnp.cos(ang), jnp.sin(ang)
    n = pos.shape[0]
    one = jnp.ones((n, HD - ROT), F32)
    zero = jnp.zeros((n, HD - ROT_HALF), F32)
    c64 = jnp.concatenate([cos, cos, one], axis=1)
    s1_64 = jnp.concatenate([-sin, zero], axis=1)
    s2_64 = jnp.concatenate([jnp.zeros((n, ROT_HALF), F32), sin, jnp.zeros((n, HD - ROT), F32)], axis=1)
    wi_scale = jnp.full((n, H_IDX), (H_IDX ** -0.5) * (D_IDX ** -0.5), F32)
    hi_c = jnp.concatenate([wi_scale, jnp.ones((n, HD - H_IDX), F32)], axis=1)
    z64 = jnp.zeros((n, HD), F32)
    return jnp.concatenate([c64, c64, s1_64, s1_64, s2_64, s2_64,
                            c64, hi_c, s1_64, z64, s2_64, z64], axis=1)


def _const_spec(shape):
    nd = len(shape)
    return pl.BlockSpec(shape, lambda *_: (0,) * nd, pipeline_mode=pl.Buffered(1))


def _half_masks(rows):
    lane = lax.broadcasted_iota(I32, (rows, PAIR), 1)
    return lane < HD


def _rms(x, g):
    return x * lax.rsqrt(jnp.mean(x * x, axis=-1, keepdims=True) + RMS_EPS) * g


def _dot(a, b):
    return jnp.dot(a, b, preferred_element_type=F32)


def _dot_nt(a, b):
    return lax.dot_general(a, b, (((1,), (1,)), ((), ())), preferred_element_type=F32)


def _dot_tn(a, b):
    return lax.dot_general(a, b, (((0,), (0,)), ((), ())), preferred_element_type=F32)


def _rope(h, c, s1, s2):
    outs = []
    for j in range(h.shape[1] // LANES):
        hj = h[:, j * LANES:(j + 1) * LANES]
        outs.append(hj * c + pltpu.roll(hj, LANES - ROT_HALF, 1) * s1 + pltpu.roll(hj, ROT_HALF, 1) * s2)
    return outs[0] if len(outs) == 1 else jnp.concatenate(outs, axis=1)


def _proj_body(offs, x_ref, g1_ref, w_ref, tab_ref, pr_o, q_o, qi_o, kd_o, vd_o, k_o, v_o, kw_o, ki2_o, g_o):
    xn = _rms(x_ref[...], g1_ref[...]).astype(BF16)
    mm = lambda name: _dot(xn, w_ref[:, offs[name][0]:offs[name][1]])
    tab = tab_ref[...]
    c, s1, s2 = (tab[:, i * LANES:(i + 1) * LANES] for i in range(3))
    ck, s1k, s2k = (tab[:, i * LANES:(i + 1) * LANES] for i in range(3, 6))
    pr_o[...] = mm("pr")
    g_o[...] = mm("g")
    v_o[...] = mm("v")
    vd_o[...] = mm("vd").astype(BF16)
    q_o[...] = _rope(mm("q"), c, s1, s2).astype(BF16)
    qi_o[...] = _rope(mm("qi"), c, s1, s2).astype(BF16)
    kd_o[...] = _rope(mm("kd"), c, s1, s2).astype(BF16)
    k_o[...] = _rope(mm("k"), c, s1, s2)
    kw_o[...] = _rope(mm("kw"), ck, s1k, s2k)
    ki2_o[...] = _rope(mm("ki2"), c, s1, s2).astype(BF16)


def _project(x, g1, w_packed, tab, tm):
    n, d = x.shape
    offs, n_cols = _group_offsets(d)
    period_tiles = tab.shape[0] // tm
    row = lambda w: pl.BlockSpec((tm, w), lambda i: (i, 0))
    widths = [(name, hi - lo) for name, (lo, hi) in offs.items()]
    dtypes = {"pr": F32, "q": BF16, "qi": BF16, "kd": BF16, "vd": BF16, "k": F32, "v": F32, "kw": F32,
              "ki2": BF16, "g": F32}
    outs = pl.pallas_call(
        functools.partial(_proj_body, offs),
        grid=(n // tm,),
        in_specs=[row(d), _const_spec((1, d)), _const_spec((d, n_cols)),
                  pl.BlockSpec((tm, tab.shape[1]), lambda i: (i % period_tiles, 0))],
        out_specs=[row(w) for _, w in widths],
        out_shape=[jax.ShapeDtypeStruct((n, w), dtypes[name]) for name, w in widths],
        compiler_params=pltpu.CompilerParams(dimension_semantics=("arbitrary",), vmem_limit_bytes=VMEM_LIMIT),
        name="in_proj",
    )(x, g1, w_packed, tab)
    return dict(zip([name for name, _ in widths], outs))


def _head_sum(x, h0):
    outs = []
    for p in range(x.shape[1] // PAIR):
        xp = x[:, p * PAIR:(p + 1) * PAIR]
        s0 = jnp.sum(jnp.where(h0, xp, 0.0), axis=1, keepdims=True)
        s1 = jnp.sum(jnp.where(h0, 0.0, xp), axis=1, keepdims=True)
        outs.append(jnp.where(h0, s0, s1))
    return jnp.concatenate(outs, axis=1)


def _rwkv_prep(chunk, valid, x, prev, mu_ref, w0_ref, ww2_ref, a0_ref, wa2_ref, wg2_ref, kk_ref, ka_ref):
    pm = x + (prev - x) * mu_ref[...]

    r = pm[:, 0:C_R]
    k = pm[:, C_R:2 * C_R]
    v = pm[:, 2 * C_R:3 * C_R]
    o = 3 * C_R
    wd = pm[:, o:o + D_W_LORA]; o += D_W_LORA
    ad = pm[:, o:o + D_A_LORA]; o += D_A_LORA
    gd = pm[:, o:o + D_G_LORA]

    z = -(w0_ref[...] + _dot(jnp.tanh(wd).astype(BF16), ww2_ref[...]))
    softplus = jnp.maximum(z, 0.0) + jnp.log1p(jnp.exp(-jnp.abs(z)))
    logdec = -jnp.exp(-softplus - 0.5)
    gate = jax.nn.sigmoid(a0_ref[...] + _dot(ad.astype(BF16), wa2_ref[...]))
    g_out = _dot(jax.nn.sigmoid(gd).astype(BF16), wg2_ref[...])

    h0 = _half_masks(chunk)
    kk = k * kk_ref[...]
    kk = kk * lax.rsqrt(jnp.maximum(_head_sum(kk * kk, h0), 1e-24))
    k2 = k * (1.0 + (gate - 1.0) * ka_ref[...])

    kk = jnp.where(valid, kk, 0.0)
    k2m = jnp.where(valid, k2, 0.0)
    logdec = jnp.where(valid, logdec, 0.0)

    ri = lax.broadcasted_iota(I32, (chunk, chunk), 0)
    ci = lax.broadcasted_iota(I32, (chunk, chunk), 1)
    tri16 = jnp.where(ri >= ci, 1.0, 0.0).astype(BF16)
    ld_hi = logdec.astype(BF16)
    ld_r = logdec - ld_hi.astype(F32)
    ld_mid = ld_r.astype(BF16)
    ld_lo = (ld_r - ld_mid.astype(F32)).astype(BF16)
    cum = _dot(tri16, ld_hi) + _dot(tri16, ld_mid) + _dot(tri16, ld_lo)
    cum_last = cum[chunk - 1:chunk, :]
    b = kk * gate
    a_t = -kk * jnp.exp(cum - logdec)
    e_neg = jnp.exp(-cum)
    b_t = b * e_neg
    k_t = k2m * e_neg
    r_t = r * jnp.exp(cum)
    e_tail = jnp.exp(cum_last - cum)
    b_g = b * e_tail
    k_g = k2m * e_tail
    g_last = jnp.exp(cum_last)
    return dict(r=r, k2=k2, v=v, g_out=g_out, a_t=a_t, b_t=b_t, k_t=k_t, r_t=r_t, b_g=b_g, k_g=k_g, g_last=g_last)


def _rwkv_body(t_real, chunk, pr_ref, sh_ref, st_ref, mu_ref, w0_ref, ww2_ref, a0_ref, wa2_ref, wg2_ref,
               kk_ref, ka_ref, rk_ref, lg_ref, lb_ref, y_ref, so_ref, s_scr, xs_scr):
    c = pl.program_id(1)
    n_chunks = pl.num_programs(1)
    n_seq = pr_ref.shape[0]
    hdr = SUBLANES

    @pl.when(c == 0)
    def _():
        s_scr[...] = st_ref[...]
        xs_scr[:, hdr - 1:hdr, :] = sh_ref[...]

    @pl.when(c > 0)
    def _():
        xs_scr[:, hdr - 1:hdr, :] = xs_scr[:, hdr + chunk - 1:hdr + chunk, :]

    valid = c * chunk + lax.broadcasted_iota(I32, (chunk, 1), 0) < t_real
    seqs = []
    for i in range(n_seq):
        x = pr_ref[i]
        xs_scr[i, hdr:hdr + chunk, :] = x
        prev = xs_scr[i, hdr - 1:hdr - 1 + chunk, :]
        seqs.append(_rwkv_prep(chunk, valid, x, prev, mu_ref, w0_ref, ww2_ref, a0_ref, wa2_ref, wg2_ref,
                               kk_ref, ka_ref))

    c2 = 2 * chunk
    units = [(i, p) for i in range(n_seq) for p in range(H_R // 2)]
    idx = range(len(units))
    h0 = _half_masks(chunk)
    split = lambda x: jnp.concatenate([jnp.where(h0, x, 0.0), jnp.where(h0, 0.0, x)], axis=0)
    split16 = lambda x: split(x).astype(BF16)
    cols = lambda x, p: x[:, p * PAIR:(p + 1) * PAIR]
    get = lambda name, u: cols(seqs[units[u][0]][name], units[u][1])
    rb = lax.broadcasted_iota(I32, (c2, c2), 0)
    cb = lax.broadcasted_iota(I32, (c2, c2), 1)
    same = (rb < chunk) == (cb < chunk)
    strict = same & (rb > cb)
    incl = same & (rb >= cb)
    eye = (rb == cb).astype(F32)
    n_sq = max(int(math.log2(chunk)) - 1, 0)

    a_s = [split16(get("a_t", u)) for u in idx]
    r_f = [split(get("r_t", u)) for u in idx]
    r_s = [r_f[u].astype(BF16) for u in idx]
    v_s = [split16(get("v", u)) for u in idx]
    bg_s = [split16(get("b_g", u)) for u in idx]
    kg_s = [split16(get("k_g", u)) for u in idx]
    xx = [_dot_nt(jnp.concatenate([a_s[u], r_s[u]], axis=0),
                  jnp.concatenate([split16(get("b_t", u)), split16(get("k_t", u))], axis=0)) for u in idx]
    l_ab = [jnp.where(strict, xx[u][0:c2, 0:c2], 0.0) for u in idx]
    l_ak = [jnp.where(strict, xx[u][0:c2, c2:], 0.0).astype(BF16) for u in idx]
    t_rb = [jnp.where(incl, xx[u][c2:, 0:c2], 0.0).astype(BF16) for u in idx]
    t_rk = [jnp.where(incl, xx[u][c2:, c2:], 0.0).astype(BF16) for u in idx]
    inv = [eye + l_ab[u] for u in idx]
    lp = l_ab
    for _ in range(n_sq):
        lp16 = [lp[u].astype(BF16) for u in idx]
        lp = [_dot(lp16[u], lp16[u]) for u in idx]
        inv = [inv[u] + _dot(inv[u].astype(BF16), lp[u].astype(BF16)) for u in idx]
    inv = [inv[u].astype(BF16) for u in idx]
    a_hat = [_dot(inv[u], a_s[u]).astype(BF16) for u in idx]
    w_s = [_dot(l_ak[u], v_s[u]).astype(BF16) for u in idx]
    u0 = [_dot(inv[u], w_s[u]).astype(BF16) for u in idx]
    r_hat = [(r_f[u] + _dot(t_rb[u], a_hat[u])).astype(BF16) for u in idx]
    y0 = [_dot(t_rb[u], u0[u]) + _dot(t_rk[u], v_s[u]) for u in idx]
    trans = [_dot_tn(a_hat[u], bg_s[u]).astype(BF16) for u in idx]
    add = [_dot_tn(u0[u], bg_s[u]) + _dot_tn(v_s[u], kg_s[u]) for u in idx]
    ys = [[] for _ in range(n_seq)]
    for u, (i, p) in enumerate(units):
        s_old = s_scr[i, p]
        s16 = s_old.astype(BF16)
        y_split = _dot_nt(r_hat[u], s16) + y0[u]
        ys[i].append(y_split[0:chunk] + y_split[chunk:])
        s_scr[i, p] = s_old * get("g_last", u) + _dot(s16, trans[u]) + add[u]

    for i in range(n_seq):
        y = jnp.concatenate(ys[i], axis=1)
        sq = seqs[i]
        mean = _head_sum(y, h0) * (1.0 / HD)
        d = y - mean
        var = _head_sum(d * d, h0) * (1.0 / HD)
        yn = d * lax.rsqrt(var + LNX_EPS) * lg_ref[...] + lb_ref[...]
        yn = yn + _head_sum(sq["r"] * sq["k2"] * rk_ref[...], h0) * sq["v"]
        y_ref[i] = (yn * sq["g_out"]).astype(y_ref.dtype)

    @pl.when(c == n_chunks - 1)
    def _():
        so_ref[...] = s_scr[...]


def _rwkv(pr, shift_prev, state_bd, lp, n_batch, t_pad, t_real, chunk, y_dtype):
    n_chunks = t_pad // chunk
    n_pairs = H_R // 2
    vec = lambda name, w: lp[name].reshape(1, w).astype(F32)
    params = [vec("rwkv_mu", P_R), vec("rwkv_w0", C_R), lp["rwkv_w_w2"].astype(BF16), vec("rwkv_a0", C_R),
              lp["rwkv_w_a2"].astype(BF16), lp["rwkv_w_g2"].astype(BF16), vec("rwkv_k_k", C_R),
              vec("rwkv_k_a", C_R), vec("rwkv_r_k", C_R), vec("rwkv_lnx_g", C_R), vec("rwkv_lnx_b", C_R)]
    n_seq = next(n for n in (4, 2, 1) if n_batch % n == 0)
    y, s_out = pl.pallas_call(
        functools.partial(_rwkv_body, t_real, chunk),
        grid=(n_batch // n_seq, n_chunks),
        in_specs=[pl.BlockSpec((n_seq, chunk, P_R), lambda b, c: (b, c, 0)),
                  pl.BlockSpec((n_seq, 1, P_R), lambda b, c: (b, 0, 0)),
                  pl.BlockSpec((n_seq, n_pairs, PAIR, PAIR), lambda b, c: (b, 0, 0, 0))]
                 + [_const_spec(p.shape) for p in params],
        out_specs=[pl.BlockSpec((n_seq, chunk, C_R), lambda b, c: (b, c, 0)),
                   pl.BlockSpec((n_seq, n_pairs, PAIR, PAIR), lambda b, c: (b, 0, 0, 0))],
        out_shape=[jax.ShapeDtypeStruct((n_batch, t_pad, C_R), y_dtype),
                   jax.ShapeDtypeStruct((n_batch, n_pairs, PAIR, PAIR), F32)],
        scratch_shapes=[pltpu.VMEM((n_seq, n_pairs, PAIR, PAIR), F32),
                        pltpu.VMEM((n_seq, SUBLANES + chunk, P_R), F32)],
        compiler_params=pltpu.CompilerParams(dimension_semantics=("arbitrary", "arbitrary"),
                                             vmem_limit_bytes=VMEM_LIMIT),
        name="rwkv7_chunked",
    )(pr.reshape(n_batch, t_pad, P_R), shift_prev.reshape(n_batch, 1, P_R), state_bd, *params)
    return y.reshape(n_batch * t_pad, C_R), s_out


def _state_to_blockdiag(s):
    b = s.shape[0]
    s = s.reshape(b, H_R // 2, 2, HD, HD)
    z = jnp.zeros_like(s[:, :, 0])
    top = jnp.concatenate([s[:, :, 0], z], axis=-1)
    bot = jnp.concatenate([z, s[:, :, 1]], axis=-1)
    return jnp.concatenate([top, bot], axis=-2)


def _state_from_blockdiag(s):
    b = s.shape[0]
    return jnp.stack([s[:, :, :HD, :HD], s[:, :, HD:, HD:]], axis=2).reshape(b, H_R, HD, HD)


def _float_of_order(biased):
    key = biased ^ jnp.int32(INT_MIN)
    return pltpu.bitcast(jnp.where(key < 0, key ^ jnp.int32(0x7FFFFFFF), key), F32)


def _topk_bias(score_scr, bias_scr, n_top):
    rows, width = score_scr.shape
    k_f = jnp.float32(n_top)
    n_parts = max(1, min(width // LANES, (8 * SUBLANES) // rows))
    bounds = [round(j * (width // LANES) / n_parts) * LANES for j in range(n_parts + 1)]

    def count_ge(thr):
        parts = [jnp.sum(jnp.where(score_scr[:, lo:hi] >= thr, 1.0, 0.0), axis=1, keepdims=True)
                 for lo, hi in zip(bounds[:-1], bounds[1:])]
        return functools.reduce(lambda a, b: a + b, parts)

    def step(it, carry):
        prefix, n_ge = carry
        trial = prefix | jnp.left_shift(jnp.int32(1), jnp.int32(31) - it)
        cnt = count_ge(_float_of_order(trial))
        take = cnt >= k_f
        return jnp.where(take, trial, prefix), jnp.where(take, cnt, n_ge)

    prefix, n_ge = lax.fori_loop(0, 32, step, (jnp.zeros((rows, 1), I32), jnp.full((rows, 1), width, F32)),
                                 unroll=4)
    lowest = jnp.float32(jnp.finfo(jnp.float32).min)
    found = (prefix ^ jnp.int32(INT_MIN)) > jnp.int32(INT_MIN + 0x7FFFFF)
    thr = jnp.where(found, _float_of_order(prefix), lowest)
    bias_scr[...] = jnp.where(score_scr[...] >= thr, 0.0, -jnp.inf)
    ambiguous = (n_ge > k_f) & found

    @pl.when(jnp.max(jnp.where(ambiguous, 1.0, 0.0)) > 0.0)
    def _():
        upper = (lax.broadcasted_iota(I32, (LANES, LANES), 0)
                 < lax.broadcasted_iota(I32, (LANES, LANES), 1)).astype(BF16)
        n_gt = jnp.sum(jnp.where(score_scr[...] > thr, 1.0, 0.0), axis=1, keepdims=True)
        need = k_f - n_gt

        def block(kb, seen):
            lo = pl.multiple_of(kb * LANES, LANES)
            sblk = score_scr[:, pl.ds(lo, LANES)]
            eq = sblk == thr
            eq16 = jnp.where(eq, 1.0, 0.0).astype(BF16)
            rank = seen + _dot(eq16, upper)
            take = (sblk > thr) | (eq & (rank < need))
            bias_scr[:, pl.ds(lo, LANES)] = jnp.where(take, 0.0, -jnp.inf)
            return seen + jnp.sum(jnp.where(eq, 1.0, 0.0), axis=1, keepdims=True)

        lax.fori_loop(0, width // LANES, block, jnp.zeros((rows, 1), F32))


def _indexer_scores(qi_ref, ki2, wi, h0):
    rows = qi_ref.shape[0]
    score = None
    for p in range(H_IDX // 2):
        qp = qi_ref[:, p * PAIR:(p + 1) * PAIR]
        zero = jnp.zeros_like(qp)
        lhs = jnp.concatenate([jnp.where(h0, qp, zero), jnp.where(h0, zero, qp)], axis=0)
        sc = jnp.maximum(_dot_nt(lhs, ki2), 0.0)
        part = wi[:, 2 * p:2 * p + 1] * sc[0:rows] + wi[:, 2 * p + 1:2 * p + 2] * sc[rows:]
        score = part if score is None else score + part
    return score


def _dsa_prompt_tile(n_top, t_keys, i, q_ref, qi_ref, kw_ref, ki2_ref, kd_ref, vd_ref, o_ref, score_scr, bias_scr):
    n_seq, tq = q_ref.shape[0], q_ref.shape[1]
    h0 = _half_masks(tq)
    q_pos = i * tq + lax.broadcasted_iota(I32, (tq, 1), 0)
    visible = lax.broadcasted_iota(I32, (tq, t_keys), 1) <= q_pos
    score_scr = score_scr.at[:, 0:t_keys]
    bias_scr = bias_scr.at[:, 0:t_keys]

    seq_rows = lambda j: pl.ds(pl.multiple_of(j * tq, tq), tq)

    @pl.loop(0, n_seq)
    def _(j):
        wi = kw_ref[j, :, D_IDX:D_IDX + H_IDX]
        score = _indexer_scores(qi_ref.at[j], ki2_ref[j, 0:t_keys, :], wi, h0)
        score_scr[seq_rows(j), :] = jnp.where(visible, score, -jnp.inf)

    _topk_bias(score_scr, bias_scr, n_top)

    scale = jnp.asarray(HD ** -0.5, BF16)

    @pl.loop(0, n_seq)
    def _(j):
        bias = bias_scr[seq_rows(j), :]
        bias2 = jnp.concatenate([bias, bias], axis=0)
        for n in range(H_KV):
            sl = slice(n * PAIR, (n + 1) * PAIR)
            qp = q_ref[j, :, sl] * scale
            zero = jnp.zeros_like(qp)
            lhs = jnp.concatenate([jnp.where(h0, qp, zero), jnp.where(h0, zero, qp)], axis=0)
            s = _dot_nt(lhs, kd_ref[j, 0:t_keys, sl]) + bias2
            m = jnp.max(s, axis=1, keepdims=True)
            p = jnp.exp(s - m)
            l = jnp.sum(p, axis=1, keepdims=True)
            o = _dot(p.astype(BF16), vd_ref[j, 0:t_keys, sl]) / l
            o_ref[j, :, sl] = jnp.where(h0, o[0:tq], o[tq:]).astype(o_ref.dtype)


def _dsa_prompt_body(n_top, n_buckets, *refs):
    tq = refs[0].shape[1]
    n_tiles = refs[3].shape[1] // tq
    i = pl.program_id(1)
    edges = [round(j * n_tiles / n_buckets) for j in range(n_buckets + 1)]
    for lo, hi in zip(edges[:-1], edges[1:]):
        if hi > lo:
            pl.when((i >= lo) & (i < hi))(functools.partial(_dsa_prompt_tile, n_top, hi * tq, i, *refs))


def _dsa_prompt(proj, n_batch, t_pad, n_top):
    nq = t_pad // Q_TILE
    n_seq = 2 if n_batch % 2 == 0 else 1
    rows = n_seq * Q_TILE
    qrow = lambda w: pl.BlockSpec((n_seq, Q_TILE, w), lambda b, i: (b, i, 0))
    seq = lambda w: pl.BlockSpec((n_seq, t_pad, w), lambda b, i: (b, 0, 0))
    per_seq = lambda name: proj[name].reshape(n_batch, t_pad, proj[name].shape[-1])
    ya = pl.pallas_call(
        functools.partial(_dsa_prompt_body, n_top, 4),
        grid=(n_batch // n_seq, nq),
        in_specs=[qrow(C_A), qrow(H_IDX * D_IDX), qrow(LANES), seq(LANES), seq(2 * C_KV), seq(2 * C_KV)],
        out_specs=qrow(C_A),
        out_shape=jax.ShapeDtypeStruct((n_batch, t_pad, C_A), BF16),
        scratch_shapes=[pltpu.VMEM((rows, t_pad), F32), pltpu.VMEM((rows, t_pad), F32)],
        compiler_params=pltpu.CompilerParams(dimension_semantics=("arbitrary", "arbitrary"),
                                             vmem_limit_bytes=VMEM_LIMIT),
        name="dsa_prompt",
    )(*[per_seq(name) for name in ("q", "qi", "kw", "ki2", "kd", "vd")])
    return ya.reshape(n_batch * t_pad, C_A)


def _ffn_body(tiles_per_seq, stride, last_tile, last_lo, x_ref, yr_ref, ya_ref, g_ref, cp_ref, wbr_ref, wba_ref,
              wo_ref, g2_ref, wup_ref, cw_ref, cb_ref, wdn_ref, gf_ref, y_ref, cl_ref, a_scr):
    i = pl.program_id(0)
    tm, d = x_ref.shape
    d_ff = cb_ref.shape[1]
    hdr = max(SUBLANES, 2 * stride)

    @pl.when(i % tiles_per_seq == 0)
    def _():
        a_scr[hdr - 2 * stride:hdr, :] = cp_ref[...]

    @pl.when(i % tiles_per_seq != 0)
    def _():
        a_scr[hdr - 2 * stride:hdr, :] = a_scr[hdr + tm - 2 * stride:hdr + tm, :]

    g = g_ref[...]
    merged = (jax.nn.sigmoid(g[:, :d]) * _dot(yr_ref[...].astype(BF16), wbr_ref[...])
              + jax.nn.sigmoid(g[:, d:]) * _dot(ya_ref[...].astype(BF16), wba_ref[...]))
    x1 = x_ref[...] + _dot(merged.astype(BF16), wo_ref[...])
    hn = _rms(x1, g2_ref[...]).astype(BF16)
    a_scr[hdr:hdr + tm, :] = _dot(hn, wup_ref[:, :d_ff])
    gate = _dot(hn, wup_ref[:, d_ff:])
    cw = cw_ref[...]
    conv = cb_ref[...] + a_scr[hdr - 2 * stride:hdr - 2 * stride + tm, :] * cw[0:1]
    conv = conv + a_scr[hdr - stride:hdr - stride + tm, :] * cw[1:2]
    conv = conv + a_scr[hdr:hdr + tm, :] * cw[2:3]
    act = 0.5 * conv * (1.0 + lax.erf(conv * (2.0 ** -0.5)))
    x2 = x1 + _dot((act * gate).astype(BF16), wdn_ref[...])
    y_ref[...] = _rms(x2, gf_ref[...])

    @pl.when(i % tiles_per_seq == last_tile)
    def _():
        cl_ref[...] = a_scr[hdr + last_lo:hdr + last_lo + 2 * stride, :]


def _merge_ffn(x, yr, ya, g, conv_prev, lp, gf, tm, tiles_per_seq, stride, t_real):
    n, d = x.shape
    d_ff = lp["conv_b"].shape[-1]
    n_seq = n // (tm * tiles_per_seq)
    first_last = (t_real - 2) * stride
    last_tile, last_lo = first_last // tm, first_last % tm
    hdr = max(SUBLANES, 2 * stride)
    row = lambda w: pl.BlockSpec((tm, w), lambda i: (i, 0))
    weights = [lp["w_br_rwkv"].astype(BF16), lp["w_br_attn"].astype(BF16), lp["w_out"].astype(BF16),
               lp["norm2_g"].reshape(1, d), lp["w_up"].astype(BF16), lp["conv_w"], lp["conv_b"].reshape(1, d_ff),
               lp["w_down"].astype(BF16), gf.reshape(1, d)]
    y, conv_last = pl.pallas_call(
        functools.partial(_ffn_body, tiles_per_seq, stride, last_tile, last_lo),
        grid=(n // tm,),
        in_specs=[row(d), row(C_R), row(C_A), row(2 * d),
                  pl.BlockSpec((None, 2 * stride, d_ff), lambda i: (i // tiles_per_seq, 0, 0))]
                 + [_const_spec(w.shape) for w in weights],
        out_specs=[row(d), pl.BlockSpec((None, 2 * stride, d_ff), lambda i: (i // tiles_per_seq, 0, 0))],
        out_shape=[jax.ShapeDtypeStruct((n, d), F32), jax.ShapeDtypeStruct((n_seq, 2 * stride, d_ff), F32)],
        scratch_shapes=[pltpu.VMEM((hdr + tm, d_ff), F32)],
        compiler_params=pltpu.CompilerParams(dimension_semantics=("arbitrary",), vmem_limit_bytes=VMEM_LIMIT),
        name="merge_convffn",
    )(x, yr, ya, g, conv_prev, *weights)
    return y, conv_last


def _page_copies(pt_ref, batch, first_page, n_pages, srcs, dsts, sems, slot):
    def copies(pg):
        page = pt_ref[batch, first_page + pg]
        rows = srcs[0].shape[-1]
        lanes = pl.ds(pl.multiple_of(pg * rows, rows), rows)
        window = lambda dst: dst.at[(slot,) + (slice(None),) * (len(dst.shape) - 2) + (lanes,)]
        return [pltpu.make_async_copy(src.at[page], window(dst), sems.at[slot, a])
                for a, (src, dst) in enumerate(zip(srcs, dsts))]

    def start():
        def one(pg, carry):
            for cp in copies(pg):
                cp.start()
            return carry
        lax.fori_loop(0, n_pages, one, 0)

    def wait():
        def one(pg, carry):
            for cp in copies(pg):
                cp.wait()
            return carry
        lax.fori_loop(0, n_pages, one, 0)

    return start, wait


def _sample_index_body(n_top, group_pages, pt_ref, qi_ref, w_ref, kin_ref, cache_ref, bias_ref,
                       kbuf, sems, score_scr, bias_scr):
    b = pl.program_id(0)
    nb = pl.num_programs(0)
    page = cache_ref.shape[-1]
    n_keys = kbuf.shape[-1]
    n_pages = n_keys // page
    n_q = score_scr.shape[0]
    slot = b % 2
    fetch = lambda bb, sl: _page_copies(pt_ref, bb, 0, n_pages, [cache_ref], [kbuf], sems, sl)

    @pl.when(b == 0)
    def _():
        fetch(0, 0)[0]()

    @pl.when(b + 1 < nb)
    def _():
        fetch(b + 1, 1 - slot)[0]()

    fetch(b, slot)[1]()

    qi = qi_ref[...]
    w = w_ref[:, 0:1]

    def head_mix(sc):
        sc = jnp.maximum(sc, 0.0) * w
        out = sc[0:n_q]
        for h in range(1, H_IDX):
            out = out + sc[h * n_q:(h + 1) * n_q]
        return out

    gk = group_pages * page
    for g in range(n_pages // group_pages):
        ki_t = kbuf[slot, :, g * gk:(g + 1) * gk].astype(BF16)
        score = head_mix(_dot(qi, ki_t))
        score_scr[:, g * gk:(g + 1) * gk] = score
    score_new = head_mix(_dot_nt(qi, kin_ref[...]))
    vis_new = lax.broadcasted_iota(I32, score_new.shape, 1) <= lax.broadcasted_iota(I32, score_new.shape, 0)
    score_scr[:, n_keys:] = jnp.where(vis_new, score_new, -jnp.inf)
    _topk_bias(score_scr, bias_scr, n_top)
    bias_ref[...] = bias_scr[...]


def _sample_attend_body(group_pages, pt_ref, q_ref, bias_ref, kn_ref, vn_ref, ck_ref, cv_ref, o_ref,
                        kbuf, vbuf, sems, m_scr, l_scr, acc_scr):
    b = pl.program_id(0)
    g = pl.program_id(1)
    nb = pl.num_programs(0)
    ng = pl.num_programs(1)
    gk = kbuf.shape[-1]
    step = b * ng + g
    slot = step % 2
    fetch = lambda bb, gg, sl: _page_copies(pt_ref, bb, gg * group_pages, group_pages, [ck_ref, cv_ref],
                                            [kbuf, vbuf], sems, sl)

    @pl.when(step == 0)
    def _():
        fetch(0, 0, 0)[0]()

    @pl.when(step + 1 < nb * ng)
    def _():
        wrap = g + 1 == ng
        fetch(jnp.where(wrap, b + 1, b), jnp.where(wrap, 0, g + 1), 1 - slot)[0]()

    fetch(b, g, slot)[1]()

    @pl.when(g == 0)
    def _():
        m_scr[...] = jnp.full(m_scr.shape, NEG_BIG, F32)
        l_scr[...] = jnp.zeros(l_scr.shape, F32)
        acc_scr[...] = jnp.zeros(acc_scr.shape, F32)

    q = q_ref[...] * jnp.asarray(HD ** -0.5, BF16)
    per_kv = q.shape[1]
    reps = per_kv // bias_ref.shape[0]

    def update(k_t, v_t, bias):
        bias_g = jnp.concatenate([bias] * reps, axis=0)
        s = jnp.concatenate([_dot(q[n], k_t[n]) + bias_g for n in range(H_KV)], axis=0)
        m_old = m_scr[...]
        m_new = jnp.maximum(m_old, jnp.max(s, axis=1, keepdims=True))
        alpha = jnp.exp(m_old - m_new)
        p = jnp.exp(s - m_new)
        l_scr[...] = alpha * l_scr[...] + jnp.sum(p, axis=1, keepdims=True)
        p16 = p.astype(BF16)
        pv = jnp.concatenate([_dot_nt(p16[n * per_kv:(n + 1) * per_kv], v_t[n]) for n in range(H_KV)], axis=0)
        acc_scr[...] = alpha * acc_scr[...] + pv
        m_scr[...] = m_new

    update(kbuf[slot].astype(BF16), vbuf[slot].astype(BF16),
           bias_ref[:, pl.ds(pl.multiple_of(g * gk, LANES), gk)])

    @pl.when(g == ng - 1)
    def _():
        update(kn_ref[...], vn_ref[...], bias_ref[:, ng * gk:])
        o_ref[...] = acc_scr[...] / l_scr[...]


def _dsa_sample(proj, cache_k, cache_v, cache_kidx, page_table, n_batch, n_q, n_top):
    n_pages = page_table.shape[1]
    n_pool, page = cache_kidx.shape[0], cache_kidx.shape[1]
    n_keys = n_pages * page
    width = n_keys + LANES
    rows = H_A * n_q
    per_q = lambda a, w: a.reshape(n_batch, n_q, w)
    heads_first = lambda a: per_q(a, H_A * HD).reshape(n_batch, n_q, H_A, HD).transpose(0, 2, 1, 3)
    pad_keys = lambda a: jnp.pad(a, ((0, 0), (0, LANES - n_q), (0, 0))).astype(BF16)

    qi = heads_first(proj["qi"]).reshape(n_batch, rows, D_IDX)
    wi = per_q(proj["kw"], LANES)[:, :, D_IDX:D_IDX + H_IDX].transpose(0, 2, 1).reshape(n_batch, rows, 1)
    wi = jnp.broadcast_to(wi, (n_batch, rows, LANES))
    ki_new = pad_keys(per_q(proj["kw"], LANES)[:, :, :D_IDX])
    idx_pages = 16
    bias = pl.pallas_call(
        functools.partial(_sample_index_body, n_top, idx_pages),
        grid_spec=pltpu.PrefetchScalarGridSpec(
            num_scalar_prefetch=1, grid=(n_batch,),
            in_specs=[pl.BlockSpec((None, rows, D_IDX), lambda b, pt: (b, 0, 0)),
                      pl.BlockSpec((None, rows, LANES), lambda b, pt: (b, 0, 0)),
                      pl.BlockSpec((None, LANES, D_IDX), lambda b, pt: (b, 0, 0)),
                      pl.BlockSpec(memory_space=pl.ANY)],
            out_specs=pl.BlockSpec((None, n_q, width), lambda b, pt: (b, 0, 0)),
            scratch_shapes=[pltpu.VMEM((2, D_IDX, n_keys), F32), pltpu.SemaphoreType.DMA((2, 1)),
                            pltpu.VMEM((n_q, width), F32), pltpu.VMEM((n_q, width), F32)]),
        out_shape=jax.ShapeDtypeStruct((n_batch, n_q, width), F32),
        compiler_params=pltpu.CompilerParams(dimension_semantics=("arbitrary",), vmem_limit_bytes=VMEM_LIMIT),
        name="dsa_sample_index",
    )(page_table, qi, wi, ki_new, cache_kidx.transpose(0, 2, 1))

    group = H_A // H_KV
    per_kv = group * n_q
    q_kv = per_q(proj["q"], C_A).reshape(n_batch, n_q, H_KV, group, HD).transpose(0, 2, 3, 1, 4)
    q_kv = q_kv.reshape(n_batch, H_KV, per_kv, HD)
    new_t = lambda a: pad_keys(per_q(a, C_KV)).reshape(n_batch, LANES, H_KV, HD).transpose(0, 2, 3, 1)
    att_pages = min(32, n_pages)
    kv_spec = pl.BlockSpec((None, H_KV, HD, LANES), lambda b, g, pt: (b, 0, 0, 0))
    o = pl.pallas_call(
        functools.partial(_sample_attend_body, att_pages),
        grid_spec=pltpu.PrefetchScalarGridSpec(
            num_scalar_prefetch=1, grid=(n_batch, n_pages // att_pages),
            in_specs=[pl.BlockSpec((None, H_KV, per_kv, HD), lambda b, g, pt: (b, 0, 0, 0)),
                      pl.BlockSpec((None, n_q, width), lambda b, g, pt: (b, 0, 0)),
                      kv_spec, kv_spec,
                      pl.BlockSpec(memory_space=pl.ANY), pl.BlockSpec(memory_space=pl.ANY)],
            out_specs=pl.BlockSpec((None, H_KV * per_kv, HD), lambda b, g, pt: (b, 0, 0)),
            scratch_shapes=[pltpu.VMEM((2, H_KV, HD, att_pages * page), F32),
                            pltpu.VMEM((2, H_KV, HD, att_pages * page), F32),
                            pltpu.SemaphoreType.DMA((2, 2)), pltpu.VMEM((rows, 1), F32), pltpu.VMEM((rows, 1), F32),
                            pltpu.VMEM((rows, HD), F32)]),
        out_shape=jax.ShapeDtypeStruct((n_batch, rows, HD), F32),
        compiler_params=pltpu.CompilerParams(dimension_semantics=("arbitrary", "arbitrary"),
                                             vmem_limit_bytes=VMEM_LIMIT),
        name="dsa_sample_attend",
    )(page_table, q_kv, bias, new_t(proj["k"]), new_t(proj["v"]),
      cache_k.transpose(0, 2, 3, 1), cache_v.transpose(0, 2, 3, 1))
    o = o.reshape(n_batch, H_KV, group, n_q, HD).transpose(0, 3, 1, 2, 4)
    return o.reshape(n_batch * n_q, C_A)


def _round_up(x, m):
    return -(-x // m) * m


def kernel(x_prompt, x_sample, cache_k, cache_v, cache_kidx, state_rwkv, state_rwkv_shift, state_ffn_conv, page_table, meta_tokens, norm1_g, w_in, rwkv_mu, rwkv_w0, rwkv_w_w2, rwkv_a0, rwkv_w_a2, rwkv_w_g2, rwkv_k_k, rwkv_k_a, rwkv_r_k, rwkv_lnx_g, rwkv_lnx_b, w_br_rwkv, w_br_attn, w_out, norm2_g, w_up, conv_w, conv_b, w_down, final_norm_g):
    assert w_in.shape[0] == 1, "single-layer model"
    n_b, seq, d = x_prompt.shape
    n_s, n_q, _ = x_sample.shape
    d_ff = conv_b.shape[-1]
    lp = {"rwkv_mu": rwkv_mu[0], "rwkv_w0": rwkv_w0[0], "rwkv_w_w2": rwkv_w_w2[0], "rwkv_a0": rwkv_a0[0],
          "rwkv_w_a2": rwkv_w_a2[0], "rwkv_w_g2": rwkv_w_g2[0], "rwkv_k_k": rwkv_k_k[0], "rwkv_k_a": rwkv_k_a[0],
          "rwkv_r_k": rwkv_r_k[0], "rwkv_lnx_g": rwkv_lnx_g[0], "rwkv_lnx_b": rwkv_lnx_b[0],
          "w_br_rwkv": w_br_rwkv[0], "w_br_attn": w_br_attn[0], "w_out": w_out[0], "norm2_g": norm2_g[0],
          "w_up": w_up[0], "conv_w": conv_w[0], "conv_b": conv_b[0], "w_down": w_down[0]}
    w_packed = _pack_w_in(w_in[0], d)
    g1 = norm1_g[0].reshape(1, d)

    t_real = seq + N_META
    t_pad = _round_up(t_real, Q_TILE)
    tiles_per_seq = 8
    tm = t_pad // tiles_per_seq
    chunk = 64
    meta = jnp.broadcast_to(meta_tokens[None].astype(x_prompt.dtype), (n_b, N_META, d))
    xp = jnp.concatenate([meta, x_prompt, jnp.zeros((n_b, t_pad - t_real, d), x_prompt.dtype)], axis=1)
    xp = xp.reshape(n_b * t_pad, d)
    proj = _project(xp, g1, w_packed, _rope_tables(jnp.arange(t_pad)), tm)
    yr, s_fin = _rwkv(proj["pr"], jnp.zeros((n_b, P_R), F32), jnp.zeros((n_b, H_R // 2, PAIR, PAIR), F32), lp,
                      n_b, t_pad, t_real, chunk, BF16)
    seq3 = lambda a: a.reshape(n_b, t_pad, a.shape[-1])
    ya = _dsa_prompt(proj, n_b, t_pad, min(TOPK_MAX, seq // 4))
    y_p, conv_p = _merge_ffn(xp, yr, ya, proj["g"], jnp.zeros((n_b, CONV_W - 1, d_ff), F32), lp, final_norm_g,
                             tm, tiles_per_seq, 1, t_real)
    out_p = (seq3(y_p)[:, N_META:t_real],
             seq3(proj["k"])[:, :t_real].reshape(1, n_b, t_real, H_KV, HD),
             seq3(proj["v"])[:, :t_real].reshape(1, n_b, t_real, H_KV, HD),
             seq3(proj["kw"])[:, :t_real, :D_IDX][None],
             _state_from_blockdiag(s_fin)[None],
             seq3(proj["pr"])[:, t_real - 1][None],
             conv_p[None])

    past = page_table.shape[1] * cache_kidx.shape[2]
    xs = x_sample.reshape(n_s * n_q, d)
    tab_s = jnp.tile(_rope_tables(past + jnp.arange(n_q)), (n_s, 1))
    proj_s = _project(xs, g1, w_packed, tab_s, n_s * n_q)
    yr_s, s_fin_s = _rwkv(proj_s["pr"], state_rwkv_shift[0], _state_to_blockdiag(state_rwkv[0]), lp,
                          n_s, n_q, n_q, n_q, F32)
    ya_s = _dsa_sample(proj_s, cache_k[0], cache_v[0], cache_kidx[0], page_table, n_s, n_q,
                       min(TOPK_MAX, (past + n_q) // 4))
    time_major = lambda a: a.reshape(n_s, n_q, a.shape[-1]).transpose(1, 0, 2).reshape(n_q * n_s, a.shape[-1])
    conv_prev_s = state_ffn_conv[0].transpose(1, 0, 2).reshape(1, (CONV_W - 1) * n_s, d_ff)
    y_s, conv_s = _merge_ffn(time_major(xs), time_major(yr_s), time_major(ya_s), time_major(proj_s["g"]),
                             conv_prev_s, lp, final_norm_g, n_s * n_q, 1, n_s, n_q)
    per_q = lambda a: a.reshape(n_s, n_q, a.shape[-1])
    out_s = (y_s.reshape(n_q, n_s, d).transpose(1, 0, 2),
             per_q(proj_s["k"]).reshape(1, n_s, n_q, H_KV, HD),
             per_q(proj_s["v"]).reshape(1, n_s, n_q, H_KV, HD),
             per_q(proj_s["kw"])[:, :, :D_IDX][None],
             _state_from_blockdiag(s_fin_s)[None],
             per_q(proj_s["pr"])[:, n_q - 1][None],
             conv_s.reshape(CONV_W - 1, n_s, d_ff).transpose(1, 0, 2)[None])
    return (out_p[0], out_s[0]) + out_p[1:] + out_s[1:]
```

```python
import functools
import math

import jax
import jax.numpy as jnp
import numpy as np
from jax import lax
from jax.experimental import pallas as pl
from jax.experimental.pallas import tpu as pltpu

F32 = jnp.float32
BF16 = jnp.bfloat16
I32 = jnp.int32

LANES = 128
SUBLANES = 8
VMEM_LIMIT = 56 * 1024 * 1024

N_META = 16
HD = 64
PAIR = 2 * HD
H_R = 8
C_R = H_R * HD
D_W_LORA, D_A_LORA, D_G_LORA = 64, 64, 128
P_R = 3 * C_R + D_W_LORA + D_A_LORA + D_G_LORA
LNX_EPS = 64e-5
H_A, H_KV = 8, 4
C_A, C_KV = H_A * HD, H_KV * HD
H_IDX, D_IDX = 8, 64
TOPK_MAX = 256
ROPE_THETA = 500000.0
ROT = HD // 4
ROT_HALF = ROT // 2
RMS_EPS = 1e-6
CONV_W = 3
Q_TILE = 128
INT_MIN = -(2 ** 31)
NEG_BIG = -1e30

_GROUPS = (("pr", P_R), ("q", C_A), ("qi", H_IDX * D_IDX), ("kd", 2 * C_KV), ("vd", 2 * C_KV),
           ("k", C_KV), ("v", C_KV), ("kw", LANES), ("ki2", LANES), ("g", None))


def _group_offsets(d_model):
    offs, o = {}, 0
    for name, width in _GROUPS:
        width = 2 * d_model if width is None else width
        offs[name] = (o, o + width)
        o += width
    return offs, o


def _pack_w_in(w_in, d_model):
    o = P_R
    q = w_in[:, o:o + C_A]; o += C_A
    k = w_in[:, o:o + C_KV]; o += C_KV
    v = w_in[:, o:o + C_KV]; o += C_KV
    qi = w_in[:, o:o + H_IDX * D_IDX]; o += H_IDX * D_IDX
    wi = w_in[:, o:o + H_IDX]; o += H_IDX
    ki = w_in[:, o:o + D_IDX]; o += D_IDX
    g = w_in[:, o:o + 2 * d_model]
    dup = lambda t: jnp.concatenate([t[:, (n // 2) * HD:(n // 2 + 1) * HD] for n in range(2 * H_KV)], axis=1)
    kw = jnp.concatenate([ki, wi, jnp.zeros((w_in.shape[0], LANES - D_IDX - H_IDX), w_in.dtype)], axis=1)
    packed = jnp.concatenate([w_in[:, :P_R], q, qi, dup(k), dup(v), k, v, kw, jnp.concatenate([ki, ki], axis=1), g],
                             axis=1)
    return packed.astype(BF16)


def _rope_tables(pos):
    inv = ROPE_THETA ** (-jnp.arange(ROT_HALF, dtype=F32) / ROT_HALF)
    ang = pos.astype(F32)[:, None] * inv[None, :]
    cos, sin = jnp.cos(ang), jnp.sin(ang)
    n = pos.shape[0]
    one = jnp.ones((n, HD - ROT), F32)
    zero = jnp.zeros((n, HD - ROT_HALF), F32)
    c64 = jnp.concatenate([cos, cos, one], axis=1)
    s1_64 = jnp.concatenate([-sin, zero], axis=1)
    s2_64 = jnp.concatenate([jnp.zeros((n, ROT_HALF), F32), sin, jnp.zeros((n, HD - ROT), F32)], axis=1)
    wi_scale = jnp.full((n, H_IDX), (H_IDX ** -0.5) * (D_IDX ** -0.5), F32)
    hi_c = jnp.concatenate([wi_scale, jnp.ones((n, HD - H_IDX), F32)], axis=1)
    z64 = jnp.zeros((n, HD), F32)
    return jnp.concatenate([c64, c64, s1_64, s1_64, s2_64, s2_64,
                            c64, hi_c, s1_64, z64, s2_64, z64], axis=1)


def _const_spec(shape):
    nd = len(shape)
    return pl.BlockSpec(shape, lambda *_: (0,) * nd, pipeline_mode=pl.Buffered(1))


def _half_masks(rows):
    lane = lax.broadcasted_iota(I32, (rows, PAIR), 1)
    return lane < HD


def _rms(x, g):
    return x * lax.rsqrt(jnp.mean(x * x, axis=-1, keepdims=True) + RMS_EPS) * g


def _dot(a, b):
    return jnp.dot(a, b, preferred_element_type=F32)


def _dot_nt(a, b):
    return lax.dot_general(a, b, (((1,), (1,)), ((), ())), preferred_element_type=F32)


def _dot_tn(a, b):
    return lax.dot_general(a, b, (((0,), (0,)), ((), ())), preferred_element_type=F32)


def _rope(h, c, s1, s2):
    outs = []
    for j in range(h.shape[1] // LANES):
        hj = h[:, j * LANES:(j + 1) * LANES]
        outs.append(hj * c + pltpu.roll(hj, LANES - ROT_HALF, 1) * s1 + pltpu.roll(hj, ROT_HALF, 1) * s2)
    return outs[0] if len(outs) == 1 else jnp.concatenate(outs, axis=1)


def _proj_body(offs, x_ref, g1_ref, w_ref, tab_ref, pr_o, q_o, qi_o, kd_o, vd_o, k_o, v_o, kw_o, ki2_o, g_o):
    xn = _rms(x_ref[...], g1_ref[...]).astype(BF16)
    mm = lambda name: _dot(xn, w_ref[:, offs[name][0]:offs[name][1]])
    tab = tab_ref[...]
    c, s1, s2 = (tab[:, i * LANES:(i + 1) * LANES] for i in range(3))
    ck, s1k, s2k = (tab[:, i * LANES:(i + 1) * LANES] for i in range(3, 6))
    pr_o[...] = mm("pr")
    g_o[...] = mm("g")
    v_o[...] = mm("v")
    vd_o[...] = mm("vd").astype(BF16)
    q_o[...] = _rope(mm("q"), c, s1, s2).astype(BF16)
    qi_o[...] = _rope(mm("qi"), c, s1, s2).astype(BF16)
    kd_o[...] = _rope(mm("kd"), c, s1, s2).astype(BF16)
    k_o[...] = _rope(mm("k"), c, s1, s2)
    kw_o[...] = _rope(mm("kw"), ck, s1k, s2k)
    ki2_o[...] = _rope(mm("ki2"), c, s1, s2).astype(BF16)


def _project(x, g1, w_packed, tab, tm):
    n, d = x.shape
    offs, n_cols = _group_offsets(d)
    period_tiles = tab.shape[0] // tm
    row = lambda w: pl.BlockSpec((tm, w), lambda i: (i, 0))
    widths = [(name, hi - lo) for name, (lo, hi) in offs.items()]
    dtypes = {"pr": F32, "q": BF16, "qi": BF16, "kd": BF16, "vd": BF16, "k": F32, "v": F32, "kw": F32,
              "ki2": BF16, "g": F32}
    outs = pl.pallas_call(
        functools.partial(_proj_body, offs),
        grid=(n // tm,),
        in_specs=[row(d), _const_spec((1, d)), _const_spec((d, n_cols)),
                  pl.BlockSpec((tm, tab.shape[1]), lambda i: (i % period_tiles, 0))],
        out_specs=[row(w) for _, w in widths],
        out_shape=[jax.ShapeDtypeStruct((n, w), dtypes[name]) for name, w in widths],
        compiler_params=pltpu.CompilerParams(dimension_semantics=("arbitrary",), vmem_limit_bytes=VMEM_LIMIT),
        name="in_proj",
    )(x, g1, w_packed, tab)
    return dict(zip([name for name, _ in widths], outs))


def _head_sum(x, h0):
    outs = []
    for p in range(x.shape[1] // PAIR):
        xp = x[:, p * PAIR:(p + 1) * PAIR]
        s0 = jnp.sum(jnp.where(h0, xp, 0.0), axis=1, keepdims=True)
        s1 = jnp.sum(jnp.where(h0, 0.0, xp), axis=1, keepdims=True)
        outs.append(jnp.where(h0, s0, s1))
    return jnp.concatenate(outs, axis=1)


def _rwkv_prep(chunk, valid, x, prev, mu_ref, w0_ref, ww2_ref, a0_ref, wa2_ref, wg2_ref, kk_ref, ka_ref):
    pm = x + (prev - x) * mu_ref[...]

    r = pm[:, 0:C_R]
    k = pm[:, C_R:2 * C_R]
    v = pm[:, 2 * C_R:3 * C_R]
    o = 3 * C_R
    wd = pm[:, o:o + D_W_LORA]; o += D_W_LORA
    ad = pm[:, o:o + D_A_LORA]; o += D_A_LORA
    gd = pm[:, o:o + D_G_LORA]

    z = -(w0_ref[...] + _dot(jnp.tanh(wd).astype(BF16), ww2_ref[...]))
    softplus = jnp.maximum(z, 0.0) + jnp.log1p(jnp.exp(-jnp.abs(z)))
    logdec = -jnp.exp(-softplus - 0.5)
    gate = jax.nn.sigmoid(a0_ref[...] + _dot(ad.astype(BF16), wa2_ref[...]))
    g_out = _dot(jax.nn.sigmoid(gd).astype(BF16), wg2_ref[...])

    h0 = _half_masks(chunk)
    kk = k * kk_ref[...]
    kk = kk * lax.rsqrt(jnp.maximum(_head_sum(kk * kk, h0), 1e-24))
    k2 = k * (1.0 + (gate - 1.0) * ka_ref[...])

    kk = jnp.where(valid, kk, 0.0)
    k2m = jnp.where(valid, k2, 0.0)
    logdec = jnp.where(valid, logdec, 0.0)

    ri = lax.broadcasted_iota(I32, (chunk, chunk), 0)
    ci = lax.broadcasted_iota(I32, (chunk, chunk), 1)
    tri16 = jnp.where(ri >= ci, 1.0, 0.0).astype(BF16)
    ld_hi = logdec.astype(BF16)
    ld_r = logdec - ld_hi.astype(F32)
    ld_mid = ld_r.astype(BF16)
    ld_lo = (ld_r - ld_mid.astype(F32)).astype(BF16)
    cum = _dot(tri16, ld_hi) + _dot(tri16, ld_mid) + _dot(tri16, ld_lo)
    cum_last = cum[chunk - 1:chunk, :]
    b = kk * gate
    a_t = -kk * jnp.exp(cum - logdec)
    e_neg = jnp.exp(-cum)
    b_t = b * e_neg
    k_t = k2m * e_neg
    r_t = r * jnp.exp(cum)
    e_tail = jnp.exp(cum_last - cum)
    b_g = b * e_tail
    k_g = k2m * e_tail
    g_last = jnp.exp(cum_last)
    return dict(r=r, k2=k2, v=v, g_out=g_out, a_t=a_t, b_t=b_t, k_t=k_t, r_t=r_t, b_g=b_g, k_g=k_g, g_last=g_last)


def _rwkv_body(t_real, chunk, pr_ref, sh_ref, st_ref, mu_ref, w0_ref, ww2_ref, a0_ref, wa2_ref, wg2_ref,
               kk_ref, ka_ref, rk_ref, lg_ref, lb_ref, y_ref, so_ref, s_scr, xs_scr):
    c = pl.program_id(1)
    n_chunks = pl.num_programs(1)
    n_seq = pr_ref.shape[0]
    hdr = SUBLANES

    @pl.when(c == 0)
    def _():
        s_scr[...] = st_ref[...]
        xs_scr[:, hdr - 1:hdr, :] = sh_ref[...]

    @pl.when(c > 0)
    def _():
        xs_scr[:, hdr - 1:hdr, :] = xs_scr[:, hdr + chunk - 1:hdr + chunk, :]

    valid = c * chunk + lax.broadcasted_iota(I32, (chunk, 1), 0) < t_real
    seqs = []
    for i in range(n_seq):
        x = pr_ref[i]
        xs_scr[i, hdr:hdr + chunk, :] = x
        prev = xs_scr[i, hdr - 1:hdr - 1 + chunk, :]
        seqs.append(_rwkv_prep(chunk, valid, x, prev, mu_ref, w0_ref, ww2_ref, a0_ref, wa2_ref, wg2_ref,
                               kk_ref, ka_ref))

    c2 = 2 * chunk
    units = [(i, p) for i in range(n_seq) for p in range(H_R // 2)]
    idx = range(len(units))
    h0 = _half_masks(chunk)
    split = lambda x: jnp.concatenate([jnp.where(h0, x, 0.0), jnp.where(h0, 0.0, x)], axis=0)
    split16 = lambda x: split(x).astype(BF16)
    cols = lambda x, p: x[:, p * PAIR:(p + 1) * PAIR]
    get = lambda name, u: cols(seqs[units[u][0]][name], units[u][1])
    rb = lax.broadcasted_iota(I32, (c2, c2), 0)
    cb = lax.broadcasted_iota(I32, (c2, c2), 1)
    same = (rb < chunk) == (cb < chunk)
    strict = same & (rb > cb)
    incl = same & (rb >= cb)
    eye = (rb == cb).astype(F32)
    n_sq = max(int(math.log2(chunk)) - 1, 0)

    a_s = [split16(get("a_t", u)) for u in idx]
    r_f = [split(get("r_t", u)) for u in idx]
    r_s = [r_f[u].astype(BF16) for u in idx]
    v_s = [split16(get("v", u)) for u in idx]
    bg_s = [split16(get("b_g", u)) for u in idx]
    kg_s = [split16(get("k_g", u)) for u in idx]
    xx = [_dot_nt(jnp.concatenate([a_s[u], r_s[u]], axis=0),
                  jnp.concatenate([split16(get("b_t", u)), split16(get("k_t", u))], axis=0)) for u in idx]
    l_ab = [jnp.where(strict, xx[u][0:c2, 0:c2], 0.0) for u in idx]
    l_ak = [jnp.where(strict, xx[u][0:c2, c2:], 0.0).astype(BF16) for u in idx]
    t_rb = [jnp.where(incl, xx[u][c2:, 0:c2], 0.0).astype(BF16) for u in idx]
    t_rk = [jnp.where(incl, xx[u][c2:, c2:], 0.0).astype(BF16) for u in idx]
    inv = [eye + l_ab[u] for u in idx]
    lp = l_ab
    for _ in range(n_sq):
        lp16 = [lp[u].astype(BF16) for u in idx]
        lp = [_dot(lp16[u], lp16[u]) for u in idx]
        inv = [inv[u] + _dot(inv[u].astype(BF16), lp[u].astype(BF16)) for u in idx]
    inv = [inv[u].astype(BF16) for u in idx]
    a_hat = [_dot(inv[u], a_s[u]).astype(BF16) for u in idx]
    w_s = [_dot(l_ak[u], v_s[u]).astype(BF16) for u in idx]
    u0 = [_dot(inv[u], w_s[u]).astype(BF16) for u in idx]
    r_hat = [(r_f[u] + _dot(t_rb[u], a_hat[u])).astype(BF16) for u in idx]
    y0 = [_dot(t_rb[u], u0[u]) + _dot(t_rk[u], v_s[u]) for u in idx]
    trans = [_dot_tn(a_hat[u], bg_s[u]).astype(BF16) for u in idx]
    add = [_dot_tn(u0[u], bg_s[u]) + _dot_tn(v_s[u], kg_s[u]) for u in idx]
    ys = [[] for _ in range(n_seq)]
    for u, (i, p) in enumerate(units):
        s_old = s_scr[i, p]
        s16 = s_old.astype(BF16)
        y_split = _dot_nt(r_hat[u], s16) + y0[u]
        ys[i].append(y_split[0:chunk] + y_split[chunk:])
        s_scr[i, p] = s_old * get("g_last", u) + _dot(s16, trans[u]) + add[u]

    for i in range(n_seq):
        y = jnp.concatenate(ys[i], axis=1)
        sq = seqs[i]
        mean = _head_sum(y, h0) * (1.0 / HD)
        d = y - mean
        var = _head_sum(d * d, h0) * (1.0 / HD)
        yn = d * lax.rsqrt(var + LNX_EPS) * lg_ref[...] + lb_ref[...]
        yn = yn + _head_sum(sq["r"] * sq["k2"] * rk_ref[...], h0) * sq["v"]
        y_ref[i] = (yn * sq["g_out"]).astype(y_ref.dtype)

    @pl.when(c == n_chunks - 1)
    def _():
        so_ref[...] = s_scr[...]


def _rwkv(pr, shift_prev, state_bd, lp, n_batch, t_pad, t_real, chunk, y_dtype):
    n_chunks = t_pad // chunk
    n_pairs = H_R // 2
    vec = lambda name, w: lp[name].reshape(1, w).astype(F32)
    params = [vec("rwkv_mu", P_R), vec("rwkv_w0", C_R), lp["rwkv_w_w2"].astype(BF16), vec("rwkv_a0", C_R),
              lp["rwkv_w_a2"].astype(BF16), lp["rwkv_w_g2"].astype(BF16), vec("rwkv_k_k", C_R),
              vec("rwkv_k_a", C_R), vec("rwkv_r_k", C_R), vec("rwkv_lnx_g", C_R), vec("rwkv_lnx_b", C_R)]
    n_seq = next(n for n in (4, 2, 1) if n_batch % n == 0)
    y, s_out = pl.pallas_call(
        functools.partial(_rwkv_body, t_real, chunk),
        grid=(n_batch // n_seq, n_chunks),
        in_specs=[pl.BlockSpec((n_seq, chunk, P_R), lambda b, c: (b, c, 0)),
                  pl.BlockSpec((n_seq, 1, P_R), lambda b, c: (b, 0, 0)),
                  pl.BlockSpec((n_seq, n_pairs, PAIR, PAIR), lambda b, c: (b, 0, 0, 0))]
                 + [_const_spec(p.shape) for p in params],
        out_specs=[pl.BlockSpec((n_seq, chunk, C_R), lambda b, c: (b, c, 0)),
                   pl.BlockSpec((n_seq, n_pairs, PAIR, PAIR), lambda b, c: (b, 0, 0, 0))],
        out_shape=[jax.ShapeDtypeStruct((n_batch, t_pad, C_R), y_dtype),
                   jax.ShapeDtypeStruct((n_batch, n_pairs, PAIR, PAIR), F32)],
        scratch_shapes=[pltpu.VMEM((n_seq, n_pairs, PAIR, PAIR), F32),
                        pltpu.VMEM((n_seq, SUBLANES + chunk, P_R), F32)],
        compiler_params=pltpu.CompilerParams(dimension_semantics=("arbitrary", "arbitrary"),
                                             vmem_limit_bytes=VMEM_LIMIT),
        name="rwkv7_chunked",
    )(pr.reshape(n_batch, t_pad, P_R), shift_prev.reshape(n_batch, 1, P_R), state_bd, *params)
    return y.reshape(n_batch * t_pad, C_R), s_out


def _state_to_blockdiag(s):
    b = s.shape[0]
    s = s.reshape(b, H_R // 2, 2, HD, HD)
    z = jnp.zeros_like(s[:, :, 0])
    top = jnp.concatenate([s[:, :, 0], z], axis=-1)
    bot = jnp.concatenate([z, s[:, :, 1]], axis=-1)
    return jnp.concatenate([top, bot], axis=-2)


def _state_from_blockdiag(s):
    b = s.shape[0]
    return jnp.stack([s[:, :, :HD, :HD], s[:, :, HD:, HD:]], axis=2).reshape(b, H_R, HD, HD)


def _float_of_order(biased):
    key = biased ^ jnp.int32(INT_MIN)
    return pltpu.bitcast(jnp.where(key < 0, key ^ jnp.int32(0x7FFFFFFF), key), F32)


def _topk_bias(score_scr, bias_scr, n_top):
    rows, width = score_scr.shape
    k_f = jnp.float32(n_top)
    n_parts = max(1, min(width // LANES, (8 * SUBLANES) // rows))
    bounds = [round(j * (width // LANES) / n_parts) * LANES for j in range(n_parts + 1)]

    def count_ge(thr):
        parts = [jnp.sum(jnp.where(score_scr[:, lo:hi] >= thr, 1.0, 0.0), axis=1, keepdims=True)
                 for lo, hi in zip(bounds[:-1], bounds[1:])]
        return functools.reduce(lambda a, b: a + b, parts)

    def step(it, carry):
        prefix, n_ge = carry
        trial = prefix | jnp.left_shift(jnp.int32(1), jnp.int32(31) - it)
        cnt = count_ge(_float_of_order(trial))
        take = cnt >= k_f
        return jnp.where(take, trial, prefix), jnp.where(take, cnt, n_ge)

    prefix, n_ge = lax.fori_loop(0, 32, step, (jnp.zeros((rows, 1), I32), jnp.full((rows, 1), width, F32)),
                                 unroll=4)
    lowest = jnp.float32(jnp.finfo(jnp.float32).min)
    found = (prefix ^ jnp.int32(INT_MIN)) > jnp.int32(INT_MIN + 0x7FFFFF)
    thr = jnp.where(found, _float_of_order(prefix), lowest)
    bias_scr[...] = jnp.where(score_scr[...] >= thr, 0.0, -jnp.inf)
    ambiguous = (n_ge > k_f) & found

    @pl.when(jnp.max(jnp.where(ambiguous, 1.0, 0.0)) > 0.0)
    def _():
        upper = (lax.broadcasted_iota(I32, (LANES, LANES), 0)
                 < lax.broadcasted_iota(I32, (LANES, LANES), 1)).astype(BF16)
        n_gt = jnp.sum(jnp.where(score_scr[...] > thr, 1.0, 0.0), axis=1, keepdims=True)
        need = k_f - n_gt

        def block(kb, seen):
            lo = pl.multiple_of(kb * LANES, LANES)
            sblk = score_scr[:, pl.ds(lo, LANES)]
            eq = sblk == thr
            eq16 = jnp.where(eq, 1.0, 0.0).astype(BF16)
            rank = seen + _dot(eq16, upper)
            take = (sblk > thr) | (eq & (rank < need))
            bias_scr[:, pl.ds(lo, LANES)] = jnp.where(take, 0.0, -jnp.inf)
            return seen + jnp.sum(jnp.where(eq, 1.0, 0.0), axis=1, keepdims=True)

        lax.fori_loop(0, width // LANES, block, jnp.zeros((rows, 1), F32))


def _indexer_scores(qi_ref, ki2, wi, h0):
    rows = qi_ref.shape[0]
    score = None
    for p in range(H_IDX // 2):
        qp = qi_ref[:, p * PAIR:(p + 1) * PAIR]
        zero = jnp.zeros_like(qp)
        lhs = jnp.concatenate([jnp.where(h0, qp, zero), jnp.where(h0, zero, qp)], axis=0)
        sc = jnp.maximum(_dot_nt(lhs, ki2), 0.0)
        part = wi[:, 2 * p:2 * p + 1] * sc[0:rows] + wi[:, 2 * p + 1:2 * p + 2] * sc[rows:]
        score = part if score is None else score + part
    return score


def _dsa_prompt_tile(n_top, t_keys, i, q_ref, qi_ref, kw_ref, ki2_ref, kd_ref, vd_ref, o_ref, score_scr, bias_scr):
    n_seq, tq = q_ref.shape[0], q_ref.shape[1]
    h0 = _half_masks(tq)
    q_pos = i * tq + lax.broadcasted_iota(I32, (tq, 1), 0)
    visible = lax.broadcasted_iota(I32, (tq, t_keys), 1) <= q_pos
    score_scr = score_scr.at[:, 0:t_keys]
    bias_scr = bias_scr.at[:, 0:t_keys]

    seq_rows = lambda j: pl.ds(pl.multiple_of(j * tq, tq), tq)
    all_selected = t_keys <= n_top

    if not all_selected:
        @pl.loop(0, n_seq)
        def _(j):
            wi = kw_ref[j, :, D_IDX:D_IDX + H_IDX]
            score = _indexer_scores(qi_ref.at[j], ki2_ref[j, 0:t_keys, :], wi, h0)
            score_scr[seq_rows(j), :] = jnp.where(visible, score, -jnp.inf)

        _topk_bias(score_scr, bias_scr, n_top)

    scale = jnp.asarray(HD ** -0.5, BF16)

    @pl.loop(0, n_seq)
    def _(j):
        bias = jnp.where(visible, 0.0, -jnp.inf) if all_selected else bias_scr[seq_rows(j), :]
        bias2 = jnp.concatenate([bias, bias], axis=0)
        for n in range(H_KV):
            sl = slice(n * PAIR, (n + 1) * PAIR)
            qp = q_ref[j, :, sl] * scale
            zero = jnp.zeros_like(qp)
            lhs = jnp.concatenate([jnp.where(h0, qp, zero), jnp.where(h0, zero, qp)], axis=0)
            s = _dot_nt(lhs, kd_ref[j, 0:t_keys, sl]) + bias2
            m = jnp.max(s, axis=1, keepdims=True)
            p = jnp.exp(s - m)
            l = jnp.sum(p, axis=1, keepdims=True)
            o = _dot(p.astype(BF16), vd_ref[j, 0:t_keys, sl]) / l
            o_ref[j, :, sl] = jnp.where(h0, o[0:tq], o[tq:]).astype(o_ref.dtype)


def _dsa_prompt_body(n_top, n_buckets, *refs):
    tq = refs[0].shape[1]
    n_tiles = refs[3].shape[1] // tq
    i = pl.program_id(1)
    n_free = min(n_top // tq, n_tiles)
    edges = [0] + [n_free + round(j * (n_tiles - n_free) / n_buckets) for j in range(n_buckets + 1)]
    for lo, hi in zip(edges[:-1], edges[1:]):
        if hi > lo:
            pl.when((i >= lo) & (i < hi))(functools.partial(_dsa_prompt_tile, n_top, hi * tq, i, *refs))


def _dsa_prompt(proj, n_batch, t_pad, n_top):
    nq = t_pad // Q_TILE
    n_seq = 2 if n_batch % 2 == 0 else 1
    rows = n_seq * Q_TILE
    qrow = lambda w: pl.BlockSpec((n_seq, Q_TILE, w), lambda b, i: (b, i, 0))
    seq = lambda w: pl.BlockSpec((n_seq, t_pad, w), lambda b, i: (b, 0, 0))
    per_seq = lambda name: proj[name].reshape(n_batch, t_pad, proj[name].shape[-1])
    ya = pl.pallas_call(
        functools.partial(_dsa_prompt_body, n_top, 4),
        grid=(n_batch // n_seq, nq),
        in_specs=[qrow(C_A), qrow(H_IDX * D_IDX), qrow(LANES), seq(LANES), seq(2 * C_KV), seq(2 * C_KV)],
        out_specs=qrow(C_A),
        out_shape=jax.ShapeDtypeStruct((n_batch, t_pad, C_A), BF16),
        scratch_shapes=[pltpu.VMEM((rows, t_pad), F32), pltpu.VMEM((rows, t_pad), F32)],
        compiler_params=pltpu.CompilerParams(dimension_semantics=("arbitrary", "arbitrary"),
                                             vmem_limit_bytes=VMEM_LIMIT),
        name="dsa_prompt",
    )(*[per_seq(name) for name in ("q", "qi", "kw", "ki2", "kd", "vd")])
    return ya.reshape(n_batch * t_pad, C_A)


def _ffn_body(tiles_per_seq, stride, last_tile, last_lo, x_ref, yr_ref, ya_ref, g_ref, cp_ref, wbr_ref, wba_ref,
              wo_ref, g2_ref, wup_ref, cw_ref, cb_ref, wdn_ref, gf_ref, y_ref, cl_ref, a_scr):
    i = pl.program_id(0)
    tm, d = x_ref.shape
    d_ff = cb_ref.shape[1]
    hdr = max(SUBLANES, 2 * stride)

    @pl.when(i % tiles_per_seq == 0)
    def _():
        a_scr[hdr - 2 * stride:hdr, :] = cp_ref[...]

    @pl.when(i % tiles_per_seq != 0)
    def _():
        a_scr[hdr - 2 * stride:hdr, :] = a_scr[hdr + tm - 2 * stride:hdr + tm, :]

    g = g_ref[...]
    merged = (jax.nn.sigmoid(g[:, :d]) * _dot(yr_ref[...].astype(BF16), wbr_ref[...])
              + jax.nn.sigmoid(g[:, d:]) * _dot(ya_ref[...].astype(BF16), wba_ref[...]))
    x1 = x_ref[...] + _dot(merged.astype(BF16), wo_ref[...])
    hn = _rms(x1, g2_ref[...]).astype(BF16)
    a_scr[hdr:hdr + tm, :] = _dot(hn, wup_ref[:, :d_ff])
    gate = _dot(hn, wup_ref[:, d_ff:])
    cw = cw_ref[...]
    conv = cb_ref[...] + a_scr[hdr - 2 * stride:hdr - 2 * stride + tm, :] * cw[0:1]
    conv = conv + a_scr[hdr - stride:hdr - stride + tm, :] * cw[1:2]
    conv = conv + a_scr[hdr:hdr + tm, :] * cw[2:3]
    act = 0.5 * conv * (1.0 + lax.erf(conv * (2.0 ** -0.5)))
    x2 = x1 + _dot((act * gate).astype(BF16), wdn_ref[...])
    y_ref[...] = _rms(x2, gf_ref[...])

    @pl.when(i % tiles_per_seq == last_tile)
    def _():
        cl_ref[...] = a_scr[hdr + last_lo:hdr + last_lo + 2 * stride, :]


def _merge_ffn(x, yr, ya, g, conv_prev, lp, gf, tm, tiles_per_seq, stride, t_real):
    n, d = x.shape
    d_ff = lp["conv_b"].shape[-1]
    n_seq = n // (tm * tiles_per_seq)
    first_last = (t_real - 2) * stride
    last_tile, last_lo = first_last // tm, first_last % tm
    hdr = max(SUBLANES, 2 * stride)
    row = lambda w: pl.BlockSpec((tm, w), lambda i: (i, 0))
    weights = [lp["w_br_rwkv"].astype(BF16), lp["w_br_attn"].astype(BF16), lp["w_out"].astype(BF16),
               lp["norm2_g"].reshape(1, d), lp["w_up"].astype(BF16), lp["conv_w"], lp["conv_b"].reshape(1, d_ff),
               lp["w_down"].astype(BF16), gf.reshape(1, d)]
    y, conv_last = pl.pallas_call(
        functools.partial(_ffn_body, tiles_per_seq, stride, last_tile, last_lo),
        grid=(n // tm,),
        in_specs=[row(d), row(C_R), row(C_A), row(2 * d),
                  pl.BlockSpec((None, 2 * stride, d_ff), lambda i: (i // tiles_per_seq, 0, 0))]
                 + [_const_spec(w.shape) for w in weights],
        out_specs=[row(d), pl.BlockSpec((None, 2 * stride, d_ff), lambda i: (i // tiles_per_seq, 0, 0))],
        out_shape=[jax.ShapeDtypeStruct((n, d), F32), jax.ShapeDtypeStruct((n_seq, 2 * stride, d_ff), F32)],
        scratch_shapes=[pltpu.VMEM((hdr + tm, d_ff), F32)],
        compiler_params=pltpu.CompilerParams(dimension_semantics=("arbitrary",), vmem_limit_bytes=VMEM_LIMIT),
        name="merge_convffn",
    )(x, yr, ya, g, conv_prev, *weights)
    return y, conv_last


def _page_copies(pt_ref, batch, first_page, n_pages, srcs, dsts, sems, slot):
    def copies(pg):
        page = pt_ref[batch, first_page + pg]
        rows = srcs[0].shape[-1]
        lanes = pl.ds(pl.multiple_of(pg * rows, rows), rows)
        window = lambda dst: dst.at[(slot,) + (slice(None),) * (len(dst.shape) - 2) + (lanes,)]
        return [pltpu.make_async_copy(src.at[page], window(dst), sems.at[slot, a])
                for a, (src, dst) in enumerate(zip(srcs, dsts))]

    def start():
        def one(pg, carry):
            for cp in copies(pg):
                cp.start()
            return carry
        lax.fori_loop(0, n_pages, one, 0)

    def wait():
        def one(pg, carry):
            for cp in copies(pg):
                cp.wait()
            return carry
        lax.fori_loop(0, n_pages, one, 0)

    return start, wait


def _sample_index_body(n_top, group_pages, pt_ref, qi_ref, w_ref, kin_ref, cache_ref, bias_ref,
                       kbuf, sems, score_scr, bias_scr):
    b = pl.program_id(0)
    nb = pl.num_programs(0)
    page = cache_ref.shape[-1]
    n_keys = kbuf.shape[-1]
    n_pages = n_keys // page
    n_q = score_scr.shape[0]
    slot = b % 2
    fetch = lambda bb, sl: _page_copies(pt_ref, bb, 0, n_pages, [cache_ref], [kbuf], sems, sl)

    @pl.when(b == 0)
    def _():
        fetch(0, 0)[0]()

    @pl.when(b + 1 < nb)
    def _():
        fetch(b + 1, 1 - slot)[0]()

    fetch(b, slot)[1]()

    qi = qi_ref[...]
    w = w_ref[:, 0:1]

    def head_mix(sc):
        sc = jnp.maximum(sc, 0.0) * w
        out = sc[0:n_q]
        for h in range(1, H_IDX):
            out = out + sc[h * n_q:(h + 1) * n_q]
        return out

    gk = group_pages * page
    for g in range(n_pages // group_pages):
        ki_t = kbuf[slot, :, g * gk:(g + 1) * gk].astype(BF16)
        score = head_mix(_dot(qi, ki_t))
        score_scr[:, g * gk:(g + 1) * gk] = score
    score_new = head_mix(_dot_nt(qi, kin_ref[...]))
    vis_new = lax.broadcasted_iota(I32, score_new.shape, 1) <= lax.broadcasted_iota(I32, score_new.shape, 0)
    score_scr[:, n_keys:] = jnp.where(vis_new, score_new, -jnp.inf)
    _topk_bias(score_scr, bias_scr, n_top)
    bias_ref[...] = bias_scr[...]


def _sample_attend_body(group_pages, pt_ref, q_ref, bias_ref, kn_ref, vn_ref, ck_ref, cv_ref, o_ref,
                        kbuf, vbuf, sems, m_scr, l_scr, acc_scr):
    b = pl.program_id(0)
    g = pl.program_id(1)
    nb = pl.num_programs(0)
    ng = pl.num_programs(1)
    gk = kbuf.shape[-1]
    step = b * ng + g
    slot = step % 2
    fetch = lambda bb, gg, sl: _page_copies(pt_ref, bb, gg * group_pages, group_pages, [ck_ref, cv_ref],
                                            [kbuf, vbuf], sems, sl)

    @pl.when(step == 0)
    def _():
        fetch(0, 0, 0)[0]()

    @pl.when(step + 1 < nb * ng)
    def _():
        wrap = g + 1 == ng
        fetch(jnp.where(wrap, b + 1, b), jnp.where(wrap, 0, g + 1), 1 - slot)[0]()

    fetch(b, g, slot)[1]()

    @pl.when(g == 0)
    def _():
        m_scr[...] = jnp.full(m_scr.shape, NEG_BIG, F32)
        l_scr[...] = jnp.zeros(l_scr.shape, F32)
        acc_scr[...] = jnp.zeros(acc_scr.shape, F32)

    q = q_ref[...] * jnp.asarray(HD ** -0.5, BF16)
    per_kv = q.shape[1]
    reps = per_kv // bias_ref.shape[0]

    def update(k_t, v_t, bias):
        bias_g = jnp.concatenate([bias] * reps, axis=0)
        s = jnp.concatenate([_dot(q[n], k_t[n]) + bias_g for n in range(H_KV)], axis=0)
        m_old = m_scr[...]
        m_new = jnp.maximum(m_old, jnp.max(s, axis=1, keepdims=True))
        alpha = jnp.exp(m_old - m_new)
        p = jnp.exp(s - m_new)
        l_scr[...] = alpha * l_scr[...] + jnp.sum(p, axis=1, keepdims=True)
        p16 = p.astype(BF16)
        pv = jnp.concatenate([_dot_nt(p16[n * per_kv:(n + 1) * per_kv], v_t[n]) for n in range(H_KV)], axis=0)
        acc_scr[...] = alpha * acc_scr[...] + pv
        m_scr[...] = m_new

    update(kbuf[slot].astype(BF16), vbuf[slot].astype(BF16),
           bias_ref[:, pl.ds(pl.multiple_of(g * gk, LANES), gk)])

    @pl.when(g == ng - 1)
    def _():
        update(kn_ref[...], vn_ref[...], bias_ref[:, ng * gk:])
        o_ref[...] = acc_scr[...] / l_scr[...]


def _dsa_sample(proj, cache_k, cache_v, cache_kidx, page_table, n_batch, n_q, n_top):
    n_pages = page_table.shape[1]
    n_pool, page = cache_kidx.shape[0], cache_kidx.shape[1]
    n_keys = n_pages * page
    width = n_keys + LANES
    rows = H_A * n_q
    per_q = lambda a, w: a.reshape(n_batch, n_q, w)
    heads_first = lambda a: per_q(a, H_A * HD).reshape(n_batch, n_q, H_A, HD).transpose(0, 2, 1, 3)
    pad_keys = lambda a: jnp.pad(a, ((0, 0), (0, LANES - n_q), (0, 0))).astype(BF16)

    qi = heads_first(proj["qi"]).reshape(n_batch, rows, D_IDX)
    wi = per_q(proj["kw"], LANES)[:, :, D_IDX:D_IDX + H_IDX].transpose(0, 2, 1).reshape(n_batch, rows, 1)
    wi = jnp.broadcast_to(wi, (n_batch, rows, LANES))
    ki_new = pad_keys(per_q(proj["kw"], LANES)[:, :, :D_IDX])
    idx_pages = 16
    bias = pl.pallas_call(
        functools.partial(_sample_index_body, n_top, idx_pages),
        grid_spec=pltpu.PrefetchScalarGridSpec(
            num_scalar_prefetch=1, grid=(n_batch,),
            in_specs=[pl.BlockSpec((None, rows, D_IDX), lambda b, pt: (b, 0, 0)),
                      pl.BlockSpec((None, rows, LANES), lambda b, pt: (b, 0, 0)),
                      pl.BlockSpec((None, LANES, D_IDX), lambda b, pt: (b, 0, 0)),
                      pl.BlockSpec(memory_space=pl.ANY)],
            out_specs=pl.BlockSpec((None, n_q, width), lambda b, pt: (b, 0, 0)),
            scratch_shapes=[pltpu.VMEM((2, D_IDX, n_keys), F32), pltpu.SemaphoreType.DMA((2, 1)),
                            pltpu.VMEM((n_q, width), F32), pltpu.VMEM((n_q, width), F32)]),
        out_shape=jax.ShapeDtypeStruct((n_batch, n_q, width), F32),
        compiler_params=pltpu.CompilerParams(dimension_semantics=("arbitrary",), vmem_limit_bytes=VMEM_LIMIT),
        name="dsa_sample_index",
    )(page_table, qi, wi, ki_new, cache_kidx.transpose(0, 2, 1))

    group = H_A // H_KV
    per_kv = group * n_q
    q_kv = per_q(proj["q"], C_A).reshape(n_batch, n_q, H_KV, group, HD).transpose(0, 2, 3, 1, 4)
    q_kv = q_kv.reshape(n_batch, H_KV, per_kv, HD)
    new_t = lambda a: pad_keys(per_q(a, C_KV)).reshape(n_batch, LANES, H_KV, HD).transpose(0, 2, 3, 1)
    att_pages = min(32, n_pages)
    kv_spec = pl.BlockSpec((None, H_KV, HD, LANES), lambda b, g, pt: (b, 0, 0, 0))
    o = pl.pallas_call(
        functools.partial(_sample_attend_body, att_pages),
        grid_spec=pltpu.PrefetchScalarGridSpec(
            num_scalar_prefetch=1, grid=(n_batch, n_pages // att_pages),
            in_specs=[pl.BlockSpec((None, H_KV, per_kv, HD), lambda b, g, pt: (b, 0, 0, 0)),
                      pl.BlockSpec((None, n_q, width), lambda b, g, pt: (b, 0, 0)),
                      kv_spec, kv_spec,
                      pl.BlockSpec(memory_space=pl.ANY), pl.BlockSpec(memory_space=pl.ANY)],
            out_specs=pl.BlockSpec((None, H_KV * per_kv, HD), lambda b, g, pt: (b, 0, 0)),
            scratch_shapes=[pltpu.VMEM((2, H_KV, HD, att_pages * page), F32),
                            pltpu.VMEM((2, H_KV, HD, att_pages * page), F32),
                            pltpu.SemaphoreType.DMA((2, 2)), pltpu.VMEM((rows, 1), F32), pltpu.VMEM((rows, 1), F32),
                            pltpu.VMEM((rows, HD), F32)]),
        out_shape=jax.ShapeDtypeStruct((n_batch, rows, HD), F32),
        compiler_params=pltpu.CompilerParams(dimension_semantics=("arbitrary", "arbitrary"),
                                             vmem_limit_bytes=VMEM_LIMIT),
        name="dsa_sample_attend",
    )(page_table, q_kv, bias, new_t(proj["k"]), new_t(proj["v"]),
      cache_k.transpose(0, 2, 3, 1), cache_v.transpose(0, 2, 3, 1))
    o = o.reshape(n_batch, H_KV, group, n_q, HD).transpose(0, 3, 1, 2, 4)
    return o.reshape(n_batch * n_q, C_A)


def _round_up(x, m):
    return -(-x // m) * m


def kernel(x_prompt, x_sample, cache_k, cache_v, cache_kidx, state_rwkv, state_rwkv_shift, state_ffn_conv, page_table, meta_tokens, norm1_g, w_in, rwkv_mu, rwkv_w0, rwkv_w_w2, rwkv_a0, rwkv_w_a2, rwkv_w_g2, rwkv_k_k, rwkv_k_a, rwkv_r_k, rwkv_lnx_g, rwkv_lnx_b, w_br_rwkv, w_br_attn, w_out, norm2_g, w_up, conv_w, conv_b, w_down, final_norm_g):
    assert w_in.shape[0] == 1, "single-layer model"
    n_b, seq, d = x_prompt.shape
    n_s, n_q, _ = x_sample.shape
    d_ff = conv_b.shape[-1]
    lp = {"rwkv_mu": rwkv_mu[0], "rwkv_w0": rwkv_w0[0], "rwkv_w_w2": rwkv_w_w2[0], "rwkv_a0": rwkv_a0[0],
          "rwkv_w_a2": rwkv_w_a2[0], "rwkv_w_g2": rwkv_w_g2[0], "rwkv_k_k": rwkv_k_k[0], "rwkv_k_a": rwkv_k_a[0],
          "rwkv_r_k": rwkv_r_k[0], "rwkv_lnx_g": rwkv_lnx_g[0], "rwkv_lnx_b": rwkv_lnx_b[0],
          "w_br_rwkv": w_br_rwkv[0], "w_br_attn": w_br_attn[0], "w_out": w_out[0], "norm2_g": norm2_g[0],
          "w_up": w_up[0], "conv_w": conv_w[0], "conv_b": conv_b[0], "w_down": w_down[0]}
    w_packed = _pack_w_in(w_in[0], d)
    g1 = norm1_g[0].reshape(1, d)

    t_real = seq + N_META
    t_pad = _round_up(t_real, Q_TILE)
    tiles_per_seq = 8
    tm = t_pad // tiles_per_seq
    chunk = 64
    meta = jnp.broadcast_to(meta_tokens[None].astype(x_prompt.dtype), (n_b, N_META, d))
    xp = jnp.concatenate([meta, x_prompt, jnp.zeros((n_b, t_pad - t_real, d), x_prompt.dtype)], axis=1)
    xp = xp.reshape(n_b * t_pad, d)
    proj = _project(xp, g1, w_packed, _rope_tables(jnp.arange(t_pad)), tm)
    yr, s_fin = _rwkv(proj["pr"], jnp.zeros((n_b, P_R), F32), jnp.zeros((n_b, H_R // 2, PAIR, PAIR), F32), lp,
                      n_b, t_pad, t_real, chunk, BF16)
    seq3 = lambda a: a.reshape(n_b, t_pad, a.shape[-1])
    ya = _dsa_prompt(proj, n_b, t_pad, min(TOPK_MAX, seq // 4))
    y_p, conv_p = _merge_ffn(xp, yr, ya, proj["g"], jnp.zeros((n_b, CONV_W - 1, d_ff), F32), lp, final_norm_g,
                             tm, tiles_per_seq, 1, t_real)
    out_p = (seq3(y_p)[:, N_META:t_real],
             seq3(proj["k"])[:, :t_real].reshape(1, n_b, t_real, H_KV, HD),
             seq3(proj["v"])[:, :t_real].reshape(1, n_b, t_real, H_KV, HD),
             seq3(proj["kw"])[:, :t_real, :D_IDX][None],
             _state_from_blockdiag(s_fin)[None],
             seq3(proj["pr"])[:, t_real - 1][None],
             conv_p[None])

    past = page_table.shape[1] * cache_kidx.shape[2]
    xs = x_sample.reshape(n_s * n_q, d)
    tab_s = jnp.tile(_rope_tables(past + jnp.arange(n_q)), (n_s, 1))
    proj_s = _project(xs, g1, w_packed, tab_s, n_s * n_q)
    yr_s, s_fin_s = _rwkv(proj_s["pr"], state_rwkv_shift[0], _state_to_blockdiag(state_rwkv[0]), lp,
                          n_s, n_q, n_q, n_q, F32)
    ya_s = _dsa_sample(proj_s, cache_k[0], cache_v[0], cache_kidx[0], page_table, n_s, n_q,
                       min(TOPK_MAX, (past + n_q) // 4))
    time_major = lambda a: a.reshape(n_s, n_q, a.shape[-1]).transpose(1, 0, 2).reshape(n_q * n_s, a.shape[-1])
    conv_prev_s = state_ffn_conv[0].transpose(1, 0, 2).reshape(1, (CONV_W - 1) * n_s, d_ff)
    y_s, conv_s = _merge_ffn(time_major(xs), time_major(yr_s), time_major(ya_s), time_major(proj_s["g"]),
                             conv_prev_s, lp, final_norm_g, n_s * n_q, 1, n_s, n_q)
    per_q = lambda a: a.reshape(n_s, n_q, a.shape[-1])
    out_s = (y_s.reshape(n_q, n_s, d).transpose(1, 0, 2),
             per_q(proj_s["k"]).reshape(1, n_s, n_q, H_KV, HD),
             per_q(proj_s["v"]).reshape(1, n_s, n_q, H_KV, HD),
             per_q(proj_s["kw"])[:, :, :D_IDX][None],
             _state_from_blockdiag(s_fin_s)[None],
             per_q(proj_s["pr"])[:, n_q - 1][None],
             conv_s.reshape(CONV_W - 1, n_s, d_ff).transpose(1, 0, 2)[None])
    return (out_p[0], out_s[0]) + out_p[1:] + out_s[1:]
```
